```python
import math
import jax, jax.numpy as jnp
from jax import lax
import numpy as np

D_MODEL = 1024
BATCH = 1
SEQ = 16384
DEPTH = 1

ATTN_Q_HEADS = 8
ATTN_KV_HEADS = 2
HEAD_DIM = 64
ATTN_WIDTH = ATTN_Q_HEADS * HEAD_DIM
KV_WIDTH = ATTN_KV_HEADS * HEAD_DIM
WINDOW = 128
ATTN_BLOCK = 128
N_REL_BUCKETS = 32
REL_MAX_DISTANCE = 128

SSM_HEADS = 8
SSM_HEAD_DIM = 64
SSM_WIDTH = SSM_HEADS * SSM_HEAD_DIM
SSM_STATE = 128
SSM_GROUPS = 2
SSM_BC = SSM_GROUPS * SSM_STATE
SSM_CONV = 4
SSM_CHUNK = 256

MIX_WIDTH = ATTN_WIDTH + SSM_WIDTH
IN_SPLITS = (ATTN_WIDTH, KV_WIDTH, KV_WIDTH, SSM_WIDTH, SSM_WIDTH, SSM_BC, SSM_BC, SSM_HEADS)
IN_PROJ_WIDTH = sum(IN_SPLITS)

D_FF = 2816
FFN_CONV = 3

NORM_EPS = 1e-6
N_MOD = 6

kernel_name = "hymba_swa_sink_ssd_convffn_adaln"


def _split_last(t, sizes):
    points = np.cumsum(sizes)[:-1].tolist()
    return jnp.split(t, points, axis=-1)


def rms_norm(x, w):
    xf = x.astype(jnp.float32)
    y = xf * lax.rsqrt(jnp.mean(xf * xf, axis=-1, keepdims=True) + NORM_EPS)
    return (y * w.astype(jnp.float32)).astype(x.dtype)


def causal_dwconv(x, w, b):
    k = w.shape[0]
    y = lax.conv_general_dilated(
        x, w[:, None, :].astype(x.dtype), window_strides=(1,), padding=[(k - 1, 0)],
        dimension_numbers=("NWC", "WIO", "NWC"), feature_group_count=x.shape[-1])
    return y + b.astype(x.dtype)


def t5_causal_bucket(dist):
    max_exact = N_REL_BUCKETS // 2
    nf = jnp.maximum(dist, 1).astype(jnp.float32)
    large = max_exact + (jnp.log(nf / max_exact) / math.log(REL_MAX_DISTANCE / max_exact)
                         * (N_REL_BUCKETS - max_exact)).astype(jnp.int32)
    large = jnp.minimum(large, N_REL_BUCKETS - 1)
    return jnp.where(dist < max_exact, dist, large)


def sliding_window_attention(q, k, v, rel_bias, sinks):
    bsz, s = q.shape[0], q.shape[1]
    L = ATTN_BLOCK
    nb = s // L
    g = ATTN_Q_HEADS // ATTN_KV_HEADS
    qb = q.reshape(bsz, nb, L, ATTN_KV_HEADS, g, HEAD_DIM)

    def band_keys(t):
        prev = jnp.pad(t, ((0, 0), (L, 0), (0, 0), (0, 0)))[:, :s]
        prev = prev.reshape(bsz, nb, L, ATTN_KV_HEADS, HEAD_DIM)
        cur = t.reshape(bsz, nb, L, ATTN_KV_HEADS, HEAD_DIM)
        return jnp.concatenate([prev, cur], axis=2)

    kb, vb = band_keys(k), band_keys(v)
    qi = jnp.arange(L, dtype=jnp.int32)[:, None] + L
    kj = jnp.arange(2 * L, dtype=jnp.int32)[None, :]
    dist = qi - kj
    band = (dist >= 0) & (dist < WINDOW)
    blk = jnp.arange(nb, dtype=jnp.int32)[:, None, None]
    valid = band[None] & ((blk > 0) | (kj[None] >= L))
    bucket = t5_causal_bucket(jnp.maximum(dist, 0))
    bias = rel_bias.astype(jnp.float32)[bucket]
    bias = bias.transpose(2, 0, 1).reshape(ATTN_KV_HEADS, g, L, 2 * L)

    scale = HEAD_DIM ** -0.5
    sc = jnp.einsum("bnqhgd,bnkhd->bnhgqk", qb, kb).astype(jnp.float32) * scale + bias[None, None]
    sc = jnp.where(valid[None, :, None, None], sc, -1e30)
    sink = sinks.astype(jnp.float32).reshape(ATTN_KV_HEADS, g)[None, None, :, :, None, None]
    m = jnp.maximum(jnp.max(sc, axis=-1, keepdims=True), sink)
    p = jnp.exp(sc - m)
    probs = p / (jnp.sum(p, axis=-1, keepdims=True) + jnp.exp(sink - m))
    o = jnp.einsum("bnhgqk,bnkhd->bnqhgd", probs.astype(v.dtype), vb)
    return o.reshape(bsz, s, ATTN_WIDTH)


def segsum(a):
    t = a.shape[-1]
    cs = jnp.cumsum(a, axis=-1)
    diff = cs[..., :, None] - cs[..., None, :]
    mask = jnp.tril(jnp.ones((t, t), dtype=bool))
    return jnp.where(mask, diff, -jnp.inf)


def ssd_chunked(x, dt, a, bm, cm):
    bsz, s = x.shape[0], x.shape[1]
    pad = (-s) % SSM_CHUNK
    if pad:
        padw = lambda t: jnp.pad(t, [(0, 0), (0, pad)] + [(0, 0)] * (t.ndim - 2))
        x, dt, bm, cm = padw(x), padw(dt), padw(bm), padw(cm)
    sp = s + pad
    nc, L = sp // SSM_CHUNK, SSM_CHUNK
    xc = (x * dt[..., None]).reshape(bsz, nc, L, SSM_HEADS, SSM_HEAD_DIM)
    adt = (dt * a).reshape(bsz, nc, L, SSM_HEADS).transpose(0, 3, 1, 2)
    bc = bm.reshape(bsz, nc, L, SSM_HEADS, SSM_STATE)
    cc = cm.reshape(bsz, nc, L, SSM_HEADS, SSM_STATE)
    a_cum = jnp.cumsum(adt, axis=-1)

    decay = jnp.exp(segsum(adt))
    cb = jnp.einsum("bclhn,bcshn->bhcls", cc, bc)
    y_diag = jnp.einsum("bhcls,bcshp->bclhp", cb * decay, xc)

    to_end = jnp.exp(a_cum[..., -1:] - a_cum).transpose(0, 2, 3, 1)
    states = jnp.einsum("bclhn,bclhp->bchpn", bc * to_end[..., None], xc)

    chunk_decay = jnp.exp(a_cum[..., -1]).transpose(2, 0, 1)

    def step(h, inp):
        dec, st = inp
        return dec[:, :, None, None] * h + st, h

    h0 = jnp.zeros((bsz, SSM_HEADS, SSM_HEAD_DIM, SSM_STATE), jnp.float32)
    _, prev = lax.scan(step, h0, (chunk_decay, states.transpose(1, 0, 2, 3, 4)))
    prev = prev.transpose(1, 0, 2, 3, 4)
    from_start = jnp.exp(a_cum).transpose(0, 2, 3, 1)
    y_off = jnp.einsum("bclhn,bchpn->bclhp", cc * from_start[..., None], prev)
    y = (y_diag + y_off).reshape(bsz, sp, SSM_HEADS, SSM_HEAD_DIM)
    return y[:, :s]


def mamba2_mixer(xs, z, bm, cm, dt_raw, conv_w, conv_b, dt_bias, a_log, d_skip, norm_w):
    bsz, s = xs.shape[0], xs.shape[1]
    xbc = jax.nn.silu(causal_dwconv(jnp.concatenate([xs, bm, cm], axis=-1), conv_w, conv_b))
    xs, bm, cm = _split_last(xbc, (SSM_WIDTH, SSM_BC, SSM_BC))
    rep = SSM_HEADS // SSM_GROUPS
    xh = xs.reshape(bsz, s, SSM_HEADS, SSM_HEAD_DIM).astype(jnp.float32)
    bh = jnp.repeat(bm.reshape(bsz, s, SSM_GROUPS, SSM_STATE), rep, axis=2).astype(jnp.float32)
    ch = jnp.repeat(cm.reshape(bsz, s, SSM_GROUPS, SSM_STATE), rep, axis=2).astype(jnp.float32)
    dt = jax.nn.softplus(dt_raw.astype(jnp.float32) + dt_bias.astype(jnp.float32))
    a = -jnp.exp(a_log.astype(jnp.float32))
    y = ssd_chunked(xh, dt, a, bh, ch) + d_skip.astype(jnp.float32)[:, None] * xh
    y = y.reshape(bsz, s, SSM_WIDTH) * jax.nn.silu(z.astype(jnp.float32))
    yg = y.reshape(bsz, s, SSM_GROUPS, SSM_WIDTH // SSM_GROUPS)
    yg = yg * lax.rsqrt(jnp.mean(yg * yg, axis=-1, keepdims=True) + NORM_EPS)
    y = yg.reshape(bsz, s, SSM_WIDTH) * norm_w.astype(jnp.float32)
    return y.astype(xs.dtype)


def setup_inputs(seed: int = 0) -> dict:
    key = jax.random.key(seed)
    ks = iter(jax.random.split(key, 32))
    f32 = jnp.float32
    nrm = lambda shape, scale: jax.random.normal(next(ks), shape, f32) * scale
    gain = lambda shape: 1.0 + nrm(shape, 0.05)

    x = nrm((BATCH, SEQ, D_MODEL), 1.0)
    c = nrm((BATCH, D_MODEL), 1.0)
    rel_bias = nrm((N_REL_BUCKETS, ATTN_Q_HEADS), 0.5)
    w_ada = nrm((DEPTH, D_MODEL, N_MOD * D_MODEL), D_MODEL ** -0.5)
    b_ada = nrm((DEPTH, N_MOD * D_MODEL), 0.02)
    pre_mix_w = gain((DEPTH, D_MODEL))
    w_in = nrm((DEPTH, D_MODEL, IN_PROJ_WIDTH), D_MODEL ** -0.5)
    attn_sinks = nrm((DEPTH, ATTN_Q_HEADS), 0.5)
    ssm_conv_w = nrm((DEPTH, SSM_CONV, SSM_WIDTH + 2 * SSM_BC), SSM_CONV ** -0.5)
    ssm_conv_b = nrm((DEPTH, SSM_WIDTH + 2 * SSM_BC), 0.02)
    dt0 = jnp.exp(jax.random.uniform(next(ks), (DEPTH, SSM_HEADS), f32,
                                     math.log(1e-3), math.log(1e-1)))
    ssm_dt_bias = dt0 + jnp.log(-jnp.expm1(-dt0))
    ssm_a_log = jnp.log(jax.random.uniform(next(ks), (DEPTH, SSM_HEADS), f32, 1.0, 16.0))
    ssm_d = gain((DEPTH, SSM_HEADS))
    ssm_norm_w = gain((DEPTH, SSM_WIDTH))
    w_out = nrm((DEPTH, MIX_WIDTH, D_MODEL), MIX_WIDTH ** -0.5)
    post_mix_w = gain((DEPTH, D_MODEL))
    pre_ffn_w = gain((DEPTH, D_MODEL))
    w_up = nrm((DEPTH, D_MODEL, 2 * D_FF), D_MODEL ** -0.5)
    ffn_conv_w = nrm((DEPTH, FFN_CONV, 2 * D_FF), FFN_CONV ** -0.5)
    ffn_conv_b = nrm((DEPTH, 2 * D_FF), 0.02)
    w_down = nrm((DEPTH, D_FF, D_MODEL), D_FF ** -0.5)
    post_ffn_w = gain((DEPTH, D_MODEL))
    return {"x": x, "c": c, "rel_bias": rel_bias, "w_ada": w_ada, "b_ada": b_ada,
            "pre_mix_w": pre_mix_w, "w_in": w_in, "attn_sinks": attn_sinks,
            "ssm_conv_w": ssm_conv_w, "ssm_conv_b": ssm_conv_b, "ssm_dt_bias": ssm_dt_bias,
            "ssm_a_log": ssm_a_log, "ssm_d": ssm_d, "ssm_norm_w": ssm_norm_w,
            "w_out": w_out, "post_mix_w": post_mix_w, "pre_ffn_w": pre_ffn_w,
            "w_up": w_up, "ffn_conv_w": ffn_conv_w, "ffn_conv_b": ffn_conv_b,
            "w_down": w_down, "post_ffn_w": post_ffn_w}


def reference(x, c, rel_bias, w_ada, b_ada, pre_mix_w, w_in, attn_sinks, ssm_conv_w,
              ssm_conv_b, ssm_dt_bias, ssm_a_log, ssm_d, ssm_norm_w, w_out, post_mix_w,
              pre_ffn_w, w_up, ffn_conv_w, ffn_conv_b, w_down, post_ffn_w):
    bsz, s = x.shape[0], x.shape[1]
    cond = jax.nn.silu(c)
    for l in range(DEPTH):
        mod = (cond @ w_ada[l] + b_ada[l])[:, None, :]
        shift1, scale1, gate1, shift2, scale2, gate2 = jnp.split(mod, N_MOD, axis=-1)

        h = rms_norm(x, pre_mix_w[l]) * (1.0 + scale1) + shift1
        proj = h @ w_in[l]
        q, k, v, xs, z, bm, cm, dt_raw = _split_last(proj, IN_SPLITS)
        attn = sliding_window_attention(
            q.reshape(bsz, s, ATTN_Q_HEADS, HEAD_DIM),
            k.reshape(bsz, s, ATTN_KV_HEADS, HEAD_DIM),
            v.reshape(bsz, s, ATTN_KV_HEADS, HEAD_DIM),
            rel_bias, attn_sinks[l])
        ssm = mamba2_mixer(xs, z, bm, cm, dt_raw, ssm_conv_w[l], ssm_conv_b[l],
                           ssm_dt_bias[l], ssm_a_log[l], ssm_d[l], ssm_norm_w[l])
        mixed = jnp.concatenate([attn, ssm], axis=-1) @ w_out[l]
        x = x + gate1 * rms_norm(mixed, post_mix_w[l])

        h = rms_norm(x, pre_ffn_w[l]) * (1.0 + scale2) + shift2
        u = causal_dwconv(h @ w_up[l], ffn_conv_w[l], ffn_conv_b[l])
        u_gate, u_val = jnp.split(u, 2, axis=-1)
        f = (jax.nn.gelu(u_gate, approximate=True) * u_val) @ w_down[l]
        x = x + gate2 * rms_norm(f, post_ffn_w[l])
    return x
```

```python
import functools
import math

import numpy as np
import jax
import jax.numpy as jnp
from jax import lax
from jax.experimental import pallas as pl
from jax.experimental.pallas import tpu as pltpu

D_MODEL = 1024
SEQ = 16384
N_MOD = 6

ATTN_Q_HEADS = 8
ATTN_KV_HEADS = 2
HEAD_DIM = 64
ATTN_WIDTH = ATTN_Q_HEADS * HEAD_DIM
KV_WIDTH = ATTN_KV_HEADS * HEAD_DIM
WINDOW = 128
ATTN_BLOCK = 128
N_REL_BUCKETS = 32
REL_MAX_DISTANCE = 128

SSM_HEADS = 8
SSM_HEAD_DIM = 64
SSM_WIDTH = SSM_HEADS * SSM_HEAD_DIM
SSM_STATE = 128
SSM_GROUPS = 2
SSM_BC = SSM_GROUPS * SSM_STATE
SSM_CONV = 4
SSM_CHUNK = 256
XBC_WIDTH = SSM_WIDTH + 2 * SSM_BC

D_FF = 2816
FFN_CONV = 3
NORM_EPS = 1e-6
MASK_VALUE = -1e30

LANES = 128
SUBLANES = 8
VMEM_LIMIT_BYTES = 56 * 1024 * 1024

TOKEN_TILE = 512
FF_BLOCK = 256
HEADS_PER_SLAB = LANES // HEAD_DIM

BF16 = jnp.bfloat16
F32 = jnp.float32


def _dot(a, b):
    return jnp.dot(a, b, preferred_element_type=F32)


def _dot_nt(a, b):
    return lax.dot_general(a, b, (((1,), (1,)), ((), ())), preferred_element_type=F32)


def _dot_tn(a, b):
    return lax.dot_general(a, b, (((0,), (0,)), ((), ())), preferred_element_type=F32)


def _split3(x):
    x1 = x.astype(BF16)
    r1 = x - x1.astype(F32)
    x2 = r1.astype(BF16)
    r2 = r1 - x2.astype(F32)
    return x1, x2, r2.astype(BF16)


def _dot_select(sel, x):
    x1, x2, x3 = _split3(x)
    return _dot(sel, x1) + _dot(sel, x2) + _dot(sel, x3)


def _select_dot(x, sel):
    x1, x2, x3 = _split3(x)
    return _dot(x1, sel) + _dot(x2, sel) + _dot(x3, sel)


def _rms(x):
    return x * lax.rsqrt(jnp.mean(x * x, axis=-1, keepdims=True) + NORM_EPS)


def _mod_kernel(c_ref, w_ref, b_ref, o_ref):
    cond = jax.nn.silu(c_ref[...])
    o_ref[...] = jnp.sum(cond * w_ref[...], axis=0, keepdims=True) + b_ref[...]


def _modulation(c_col, w_ada, b_ada):
    return pl.pallas_call(
        _mod_kernel,
        grid=(N_MOD,),
        in_specs=[
            pl.BlockSpec((D_MODEL, 1), lambda j: (0, 0)),
            pl.BlockSpec((D_MODEL, D_MODEL), lambda j: (0, j)),
            pl.BlockSpec((1, D_MODEL), lambda j: (0, j)),
        ],
        out_specs=pl.BlockSpec((1, D_MODEL), lambda j: (0, j)),
        out_shape=jax.ShapeDtypeStruct((1, N_MOD * D_MODEL), F32),
        name="mod",
    )(c_col, w_ada, b_ada)


def _t5_bucket_table():
    L = ATTN_BLOCK
    dist = np.maximum((np.arange(L)[:, None] + L) - np.arange(2 * L)[None, :], 0)
    max_exact = N_REL_BUCKETS // 2
    nf = np.maximum(dist, 1).astype(np.float64)
    large = max_exact + (np.log(nf / max_exact) / math.log(REL_MAX_DISTANCE / max_exact)
                         * (N_REL_BUCKETS - max_exact)).astype(np.int32)
    large = np.minimum(large, N_REL_BUCKETS - 1)
    return np.where(dist < max_exact, dist, large).astype(np.int32)


def _bias_kernel(rb_ref, bucket_ref, o_ref):
    L = ATTN_BLOCK
    bucket = bucket_ref[...]
    qi = lax.broadcasted_iota(jnp.int32, (L, 2 * L), 0) + L
    kj = lax.broadcasted_iota(jnp.int32, (L, 2 * L), 1)
    dist = qi - kj
    for h in range(ATTN_Q_HEADS):
        acc = jnp.zeros((L, 2 * L), F32)
        for b in range(N_REL_BUCKETS):
            acc = jnp.where(bucket == b, rb_ref[b, h], acc)
        band = jnp.where(dist >= 0, jnp.where(dist < WINDOW, acc, MASK_VALUE), MASK_VALUE)
        slab, half = divmod(h, HEADS_PER_SLAB)
        cols = slice(half * 2 * L, (half + 1) * 2 * L)
        o_ref[1, slab, :, cols] = band
        o_ref[0, slab, :, cols] = jnp.where(kj >= L, band, MASK_VALUE)


def _bias_table(rel_bias):
    L = ATTN_BLOCK
    n_slabs = ATTN_Q_HEADS // HEADS_PER_SLAB
    return pl.pallas_call(
        _bias_kernel,
        in_specs=[
            pl.BlockSpec(memory_space=pltpu.SMEM),
            pl.BlockSpec((L, 2 * L), lambda: (0, 0)),
        ],
        out_specs=pl.BlockSpec((2, n_slabs, L, HEADS_PER_SLAB * 2 * L), lambda: (0, 0, 0, 0)),
        out_shape=jax.ShapeDtypeStruct((2, n_slabs, L, HEADS_PER_SLAB * 2 * L), F32),
        name="bias",
    )(rel_bias, jnp.asarray(_t5_bucket_table()))


def _dup_heads(t):
    lo = lax.broadcasted_iota(jnp.int32, t.shape, 1) < HEAD_DIM
    swapped = pltpu.roll(t, HEAD_DIM, axis=1)
    return jnp.where(lo, t, swapped), jnp.where(lo, swapped, t)


def _inproj_kernel(x_ref, nw_ref, sc_ref, sh_ref, wq_ref, wkv_ref, wxbc_ref, wz_ref, wdt_ref,
                   q_ref, kv_ref, xbc_ref, z_ref, dt_ref):
    h = _rms(x_ref[...]) * nw_ref[...]
    h = (h * (1.0 + sc_ref[...]) + sh_ref[...]).astype(BF16)
    q_ref[...] = (_dot(h, wq_ref[...]) * (HEAD_DIM ** -0.5)).astype(BF16)
    kv = _dot(h, wkv_ref[...])
    k0, k1 = _dup_heads(kv[:, :KV_WIDTH])
    v0, v1 = _dup_heads(kv[:, KV_WIDTH:])
    kv_ref[:, 0 * LANES:1 * LANES] = k0.astype(BF16)
    kv_ref[:, 1 * LANES:2 * LANES] = k1.astype(BF16)
    kv_ref[:, 2 * LANES:3 * LANES] = v0.astype(BF16)
    kv_ref[:, 3 * LANES:4 * LANES] = v1.astype(BF16)
    xbc_ref[...] = _dot(h, wxbc_ref[...])
    z_ref[...] = _dot(h, wz_ref[...])
    dt_ref[...] = _dot(h, wdt_ref[...])


def _in_proj(x, nw, scale, shift, wq, wkv, wxbc, wz, wdt):
    tm = TOKEN_TILE
    row = lambda i: (i, 0)
    fixed = lambda i: (0, 0)
    vec = pl.BlockSpec((1, D_MODEL), fixed)
    full = lambda a: pl.BlockSpec(a.shape, fixed)
    return pl.pallas_call(
        _inproj_kernel,
        grid=(SEQ // tm,),
        in_specs=[pl.BlockSpec((tm, D_MODEL), row), vec, vec, vec,
                  full(wq), full(wkv), full(wxbc), full(wz), full(wdt)],
        out_specs=[
            pl.BlockSpec((tm, ATTN_WIDTH), row),
            pl.BlockSpec((tm, 4 * LANES), row),
            pl.BlockSpec((tm, XBC_WIDTH), row),
            pl.BlockSpec((tm, SSM_WIDTH), row),
            pl.BlockSpec((tm, LANES), row),
        ],
        out_shape=[
            jax.ShapeDtypeStruct((SEQ, ATTN_WIDTH), BF16),
            jax.ShapeDtypeStruct((SEQ, 4 * LANES), BF16),
            jax.ShapeDtypeStruct((SEQ, XBC_WIDTH), F32),
            jax.ShapeDtypeStruct((SEQ, SSM_WIDTH), F32),
            jax.ShapeDtypeStruct((SEQ, LANES), F32),
        ],
        compiler_params=pltpu.CompilerParams(
            dimension_semantics=("parallel",), vmem_limit_bytes=VMEM_LIMIT_BYTES),
        name="in_proj",
    )(x, nw, scale, shift, wq, wkv, wxbc, wz, wdt)


def _attn_kernel(sink_ref, q_ref, kv_ref, kvprev_ref, bias_ref, o_ref, band_ref):
    L = ATTN_BLOCK
    i = pl.program_id(0)
    band_ref[0:L, :] = kvprev_ref[...]
    band_ref[L:, :] = kv_ref[...]
    lane_lo = lax.broadcasted_iota(jnp.int32, (2 * L, LANES), 1) < HEAD_DIM
    out_lo = lax.broadcasted_iota(jnp.int32, (L, LANES), 1) < HEAD_DIM
    zero = jnp.zeros((2 * L, LANES), BF16)
    heads_per_kv = ATTN_Q_HEADS // ATTN_KV_HEADS

    def block(b, carry):
        r0 = pl.multiple_of(b * L, L)
        table = jnp.where(jnp.logical_and(i == 0, b == 0), 0, 1)
        for kvh in range(ATTN_KV_HEADS):
            kk = band_ref[pl.ds(r0, 2 * L), kvh * LANES:(kvh + 1) * LANES]
            vv = band_ref[pl.ds(r0, 2 * L), (2 + kvh) * LANES:(3 + kvh) * LANES]
            k2 = jnp.concatenate([jnp.where(lane_lo, kk, zero), jnp.where(lane_lo, zero, kk)], axis=0)
            v2 = jnp.concatenate([jnp.where(lane_lo, vv, zero), jnp.where(lane_lo, zero, vv)], axis=0)
            for s in range(heads_per_kv // HEADS_PER_SLAB):
                slab = kvh * (heads_per_kv // HEADS_PER_SLAB) + s
                qs = q_ref[pl.ds(r0, L), slab * LANES:(slab + 1) * LANES]
                sc = _dot_nt(qs, k2) + bias_ref[table, slab]
                ps, inv = [], []
                for half in range(HEADS_PER_SLAB):
                    sink = sink_ref[0, slab * HEADS_PER_SLAB + half]
                    sh = sc[:, half * 2 * L:(half + 1) * 2 * L]
                    m = jnp.maximum(jnp.max(sh, axis=-1, keepdims=True), sink)
                    p = jnp.exp(sh - m)
                    denom = jnp.sum(p, axis=-1, keepdims=True) + jnp.exp(sink - m)
                    ps.append(p.astype(BF16))
                    inv.append(1.0 / denom)
                o = _dot(jnp.concatenate(ps, axis=1), v2)
                o = o * jnp.where(out_lo, inv[0], inv[1])
                o_ref[pl.ds(r0, L), slab * LANES:(slab + 1) * LANES] = o.astype(o_ref.dtype)
        return carry

    lax.fori_loop(0, TOKEN_TILE // L, block, 0)


def _attention(sinks, q, kv, bias):
    tm, L = TOKEN_TILE, ATTN_BLOCK
    blocks_per_tile = tm // L
    row = lambda i: (i, 0)
    return pl.pallas_call(
        _attn_kernel,
        grid=(SEQ // tm,),
        in_specs=[
            pl.BlockSpec(memory_space=pltpu.SMEM),
            pl.BlockSpec((tm, ATTN_WIDTH), row),
            pl.BlockSpec((tm, 4 * LANES), row),
            pl.BlockSpec((L, 4 * LANES), lambda i: (jnp.maximum(i * blocks_per_tile - 1, 0), 0)),
            pl.BlockSpec(bias.shape, lambda i: (0, 0, 0, 0)),
        ],
        out_specs=pl.BlockSpec((tm, ATTN_WIDTH), row),
        out_shape=jax.ShapeDtypeStruct((SEQ, ATTN_WIDTH), BF16),
        scratch_shapes=[pltpu.VMEM((L + tm, 4 * LANES), BF16)],
        compiler_params=pltpu.CompilerParams(
            dimension_semantics=("parallel",), vmem_limit_bytes=VMEM_LIMIT_BYTES),
        name="attn",
    )(sinks, q, kv, kv, bias)


def _ssd_kernel(xbc_ref, halo_ref, z_ref, dt_ref, cw_ref, cb_ref, dtb_ref, alog_ref, dskip_ref,
                nw_ref, o_ref, xpad_ref, state_ref):
    T = SSM_CHUNK
    c = pl.program_id(0)

    @pl.when(c == 0)
    def _():
        state_ref[...] = jnp.zeros_like(state_ref)

    xpad_ref[0:SUBLANES, :] = jnp.where(c > 0, halo_ref[...], 0.0)
    xpad_ref[SUBLANES:, :] = xbc_ref[...]
    acc = cb_ref[...]
    for k in range(SSM_CONV):
        off = SUBLANES - (SSM_CONV - 1) + k
        acc = acc + cw_ref[k:k + 1, :] * xpad_ref[off:off + T, :]
    act = jax.nn.silu(acc)
    xs = act[:, :SSM_WIDTH]

    dt = jax.nn.softplus(dt_ref[...] + dtb_ref[...])
    adt = dt * (-jnp.exp(alog_ref[...]))
    ti = lax.broadcasted_iota(jnp.int32, (T, T), 0)
    tj = lax.broadcasted_iota(jnp.int32, (T, T), 1)
    causal = ti >= tj
    acum = _dot_select(jnp.where(causal, 1.0, 0.0).astype(BF16), adt)
    acum_t = acum.T

    er = lax.broadcasted_iota(jnp.int32, (LANES, SSM_WIDTH), 0)
    ec = lax.broadcasted_iota(jnp.int32, (LANES, SSM_WIDTH), 1)
    expand = jnp.where(jnp.logical_and(ec >= er * SSM_HEAD_DIM, ec < (er + 1) * SSM_HEAD_DIM),
                       1.0, 0.0).astype(BF16)
    acum_e = _select_dot(acum, expand)
    dt_e = _select_dot(dt, expand)
    last_e = acum_e[T - 1:T, :]
    from_start = jnp.exp(acum_e)
    to_end = jnp.exp(last_e - acum_e)
    chunk_decay = jnp.exp(last_e)

    xc = xs * dt_e
    xc_b = xc.astype(BF16)
    xce_b = (xc * to_end).astype(BF16)
    lane_lo = lax.broadcasted_iota(jnp.int32, (T, LANES), 1) < SSM_HEAD_DIM
    heads_per_group = SSM_HEADS // SSM_GROUPS
    gw = heads_per_group * SSM_HEAD_DIM

    y_parts = []
    for g in range(SSM_GROUPS):
        b_g = act[:, SSM_WIDTH + g * SSM_STATE:SSM_WIDTH + (g + 1) * SSM_STATE].astype(BF16)
        c_g = act[:, SSM_WIDTH + SSM_BC + g * SSM_STATE:
                  SSM_WIDTH + SSM_BC + (g + 1) * SSM_STATE].astype(BF16)
        cbm = _dot_nt(c_g, b_g)
        gcols = slice(g * gw, (g + 1) * gw)
        y_off = _dot(c_g, state_ref[:, gcols].astype(BF16)) * from_start[:, gcols]
        new_states = _dot_tn(b_g, xce_b[:, gcols])
        state_ref[:, gcols] = state_ref[:, gcols] * chunk_decay[:, gcols] + new_states
        for s in range(heads_per_group // HEADS_PER_SLAB):
            slab = g * (heads_per_group // HEADS_PER_SLAB) + s
            x_slab = xc_b[:, slab * LANES:(slab + 1) * LANES]
            halves = []
            for half in range(HEADS_PER_SLAB):
                h = slab * HEADS_PER_SLAB + half
                diff = acum[:, h:h + 1] - acum_t[h:h + 1, :]
                decay = jnp.exp(jnp.where(causal, diff, MASK_VALUE))
                halves.append(_dot((cbm * decay).astype(BF16), x_slab))
            y_parts.append(jnp.where(lane_lo, halves[0], halves[1])
                           + y_off[:, s * LANES:(s + 1) * LANES])
    y = jnp.concatenate(y_parts, axis=1) + dskip_ref[...] * xs
    y = y * jax.nn.silu(z_ref[...])
    norm_w = nw_ref[...]
    gn = SSM_WIDTH // SSM_GROUPS
    for g in range(SSM_GROUPS):
        cols = slice(g * gn, (g + 1) * gn)
        o_ref[:, cols] = (_rms(y[:, cols]) * norm_w[:, cols]).astype(o_ref.dtype)


def _ssd(xbc, z, dt_raw, conv_w, conv_b, dt_bias, a_log, d_skip, norm_w):
    T = SSM_CHUNK
    row = lambda c: (c, 0)
    fixed = lambda c: (0, 0)
    full = lambda a: pl.BlockSpec(a.shape, fixed)
    return pl.pallas_call(
        _ssd_kernel,
        grid=(SEQ // T,),
        in_specs=[
            pl.BlockSpec((T, XBC_WIDTH), row),
            pl.BlockSpec((SUBLANES, XBC_WIDTH), lambda c: (jnp.maximum(c * (T // SUBLANES) - 1, 0), 0)),
            pl.BlockSpec((T, SSM_WIDTH), row),
            pl.BlockSpec((T, LANES), row),
            full(conv_w), full(conv_b), full(dt_bias), full(a_log), full(d_skip), full(norm_w),
        ],
        out_specs=pl.BlockSpec((T, SSM_WIDTH), row),
        out_shape=jax.ShapeDtypeStruct((SEQ, SSM_WIDTH), BF16),
        scratch_shapes=[
            pltpu.VMEM((SUBLANES + T, XBC_WIDTH), F32),
            pltpu.VMEM((SSM_STATE, SSM_WIDTH), F32),
        ],
        compiler_params=pltpu.CompilerParams(
            dimension_semantics=("arbitrary",), vmem_limit_bytes=VMEM_LIMIT_BYTES),
        name="ssd",
    )(xbc, xbc, z, dt_raw, conv_w, conv_b, dt_bias, a_log, d_skip, norm_w)


def _ffn_kernel(x_ref, a_ref, s_ref, g1_ref, pmw_ref, pfw_ref, sc2_ref, sh2_ref, g2_ref, pow_ref,
                wout_ref, wup_ref, cw_ref, cb_ref, wdown_ref, o_ref,
                h_ref, acc_ref, ubuf_ref, carry_ref):
    tm = TOKEN_TILE
    i = pl.program_id(0)

    @pl.when(i == 0)
    def _():
        carry_ref[...] = jnp.zeros_like(carry_ref)

    mixed = _dot(a_ref[...], wout_ref[:ATTN_WIDTH, :]) + _dot(s_ref[...], wout_ref[ATTN_WIDTH:, :])
    x1 = x_ref[...] + g1_ref[...] * (_rms(mixed) * pmw_ref[...])
    o_ref[...] = x1
    h = _rms(x1) * pfw_ref[...]
    h_ref[...] = (h * (1.0 + sc2_ref[...]) + sh2_ref[...]).astype(BF16)
    acc_ref[...] = jnp.zeros_like(acc_ref)

    def conv(slot, u, cols):
        ubuf_ref[slot, 0:SUBLANES, :] = carry_ref[:, cols]
        ubuf_ref[slot, SUBLANES:, :] = u
        carry_ref[:, cols] = u[tm - SUBLANES:, :]
        out = cb_ref[:, cols] + cw_ref[FFN_CONV - 1:FFN_CONV, cols] * u
        for k in range(FFN_CONV - 1):
            off = SUBLANES - (FFN_CONV - 1) + k
            out = out + cw_ref[k:k + 1, cols] * ubuf_ref[slot, off:off + tm, :]
        return out

    for j in range(D_FF // FF_BLOCK):
        gcols = slice(j * FF_BLOCK, (j + 1) * FF_BLOCK)
        vcols = slice(D_FF + j * FF_BLOCK, D_FF + (j + 1) * FF_BLOCK)
        hb = h_ref[...]
        u_gate = conv(0, _dot(hb, wup_ref[:, gcols]), gcols)
        u_val = conv(1, _dot(hb, wup_ref[:, vcols]), vcols)
        f = (jax.nn.gelu(u_gate, approximate=True) * u_val).astype(BF16)
        acc_ref[...] += _dot(f, wdown_ref[gcols, :])

    o_ref[...] = o_ref[...] + g2_ref[...] * (_rms(acc_ref[...]) * pow_ref[...])


def _out_ffn(x, attn, ssm, g1, pmw, pfw, sc2, sh2, g2, pow_, w_out, w_up, cw, cb, w_down):
    tm = TOKEN_TILE
    row = lambda i: (i, 0)
    fixed = lambda i: (0, 0)
    vec = pl.BlockSpec((1, D_MODEL), fixed)
    resident = lambda a: pl.BlockSpec(a.shape, fixed, pipeline_mode=pl.Buffered(1))
    return pl.pallas_call(
        _ffn_kernel,
        grid=(SEQ // tm,),
        in_specs=[
            pl.BlockSpec((tm, D_MODEL), row),
            pl.BlockSpec((tm, ATTN_WIDTH), row),
            pl.BlockSpec((tm, SSM_WIDTH), row),
            vec, vec, vec, vec, vec, vec, vec,
            resident(w_out), resident(w_up), resident(cw), resident(cb), resident(w_down),
        ],
        out_specs=pl.BlockSpec((tm, D_MODEL), row),
        out_shape=jax.ShapeDtypeStruct((SEQ, D_MODEL), F32),
        scratch_shapes=[
            pltpu.VMEM((tm, D_MODEL), BF16),
            pltpu.VMEM((tm, D_MODEL), F32),
            pltpu.VMEM((2, SUBLANES + tm, FF_BLOCK), F32),
            pltpu.VMEM((SUBLANES, 2 * D_FF), F32),
        ],
        compiler_params=pltpu.CompilerParams(
            dimension_semantics=("arbitrary",), vmem_limit_bytes=VMEM_LIMIT_BYTES),
        name="ffn",
    )(x, attn, ssm, g1, pmw, pfw, sc2, sh2, g2, pow_, w_out, w_up, cw, cb, w_down)


def _pad_lanes(a):
    return jnp.pad(a, ((0, 0), (0, LANES - a.shape[-1])))


def kernel(x, c, rel_bias, w_ada, b_ada, pre_mix_w, w_in, attn_sinks, ssm_conv_w, ssm_conv_b,
           ssm_dt_bias, ssm_a_log, ssm_d, ssm_norm_w, w_out, post_mix_w, pre_ffn_w, w_up,
           ffn_conv_w, ffn_conv_b, w_down, post_ffn_w):
    bsz, s, d = x.shape
    assert (bsz, s, d) == (1, SEQ, D_MODEL) and w_ada.shape[0] == 1
    x2 = x.reshape(s, d)

    mod = _modulation(c.reshape(d, 1), w_ada[0], b_ada[0].reshape(1, -1))
    shift1, scale1, gate1, shift2, scale2, gate2 = jnp.split(mod, N_MOD, axis=-1)
    bias = _bias_table(rel_bias)

    o_q, o_k, o_v, o_xs, o_z, o_b, o_c, o_dt = np.cumsum(
        (0, ATTN_WIDTH, KV_WIDTH, KV_WIDTH, SSM_WIDTH, SSM_WIDTH, SSM_BC, SSM_BC)).tolist()
    wi = w_in[0]
    wq = wi[:, o_q:o_k].astype(BF16)
    wkv = wi[:, o_k:o_xs].astype(BF16)
    wxbc = jnp.concatenate([wi[:, o_xs:o_z], wi[:, o_b:o_dt]], axis=1).astype(BF16)
    wz = wi[:, o_z:o_b].astype(BF16)
    wdt = _pad_lanes(wi[:, o_dt:]).astype(BF16)
    q, kv, xbc, z, dt_raw = _in_proj(x2, pre_mix_w, scale1, shift1, wq, wkv, wxbc, wz, wdt)

    attn = _attention(attn_sinks, q, kv, bias)
    ssm = _ssd(xbc, z, dt_raw, ssm_conv_w[0], ssm_conv_b, _pad_lanes(ssm_dt_bias),
               _pad_lanes(ssm_a_log), jnp.repeat(ssm_d, SSM_HEAD_DIM, axis=1), ssm_norm_w)

    out = _out_ffn(x2, attn, ssm, gate1, post_mix_w, pre_ffn_w, scale2, shift2, gate2, post_ffn_w,
                   w_out[0].astype(BF16), w_up[0].astype(BF16), ffn_conv_w[0], ffn_conv_b,
                   w_down[0].astype(BF16))
    return out.reshape(bsz, s, d)
```

```python
import math

import numpy as np
import jax
import jax.numpy as jnp
from jax import lax
from jax.experimental import pallas as pl
from jax.experimental.pallas import tpu as pltpu

D_MODEL = 1024
SEQ = 16384
N_MOD = 6

ATTN_Q_HEADS = 8
ATTN_KV_HEADS = 2
HEAD_DIM = 64
ATTN_WIDTH = ATTN_Q_HEADS * HEAD_DIM
KV_WIDTH = ATTN_KV_HEADS * HEAD_DIM
WINDOW = 128
ATTN_BLOCK = 128
N_REL_BUCKETS = 32
REL_MAX_DISTANCE = 128

SSM_HEADS = 8
SSM_HEAD_DIM = 64
SSM_WIDTH = SSM_HEADS * SSM_HEAD_DIM
SSM_STATE = 128
SSM_GROUPS = 2
SSM_BC = SSM_GROUPS * SSM_STATE
SSM_CONV = 4
SSM_CHUNK = 256
XBC_WIDTH = SSM_WIDTH + 2 * SSM_BC

D_FF = 2816
FFN_CONV = 3
NORM_EPS = 1e-6
MASK_VALUE = -1e30

LANES = 128
SUBLANES = 8
VMEM_LIMIT_BYTES = 56 * 1024 * 1024

TOKEN_TILE = 512
FF_BLOCK = 256
HEADS_PER_SLAB = LANES // HEAD_DIM
SEG = TOKEN_TILE // SUBLANES
SEG_PITCH = SEG + SUBLANES

BF16 = jnp.bfloat16
F32 = jnp.float32


def _dot(a, b):
    return jnp.dot(a, b, preferred_element_type=F32)


def _dot_nt(a, b):
    return lax.dot_general(a, b, (((1,), (1,)), ((), ())), preferred_element_type=F32)


def _dot_tn(a, b):
    return lax.dot_general(a, b, (((0,), (0,)), ((), ())), preferred_element_type=F32)


def _split3(x):
    x1 = x.astype(BF16)
    r1 = x - x1.astype(F32)
    x2 = r1.astype(BF16)
    r2 = r1 - x2.astype(F32)
    return x1, x2, r2.astype(BF16)


def _dot_select(sel, x):
    x1, x2, x3 = _split3(x)
    return _dot(sel, x1) + _dot(sel, x2) + _dot(sel, x3)


def _select_dot(x, sel):
    x1, x2, x3 = _split3(x)
    return _dot(x1, sel) + _dot(x2, sel) + _dot(x3, sel)


def _rms(x):
    return x * lax.rsqrt(jnp.mean(x * x, axis=-1, keepdims=True) + NORM_EPS)


def _mod_kernel(c_ref, w_ref, b_ref, o_ref):
    cond = jax.nn.silu(c_ref[...])
    o_ref[...] = jnp.sum(cond * w_ref[...], axis=0, keepdims=True) + b_ref[...]


def _modulation(c_col, w_ada, b_ada):
    return pl.pallas_call(
        _mod_kernel,
        grid=(N_MOD,),
        in_specs=[
            pl.BlockSpec((D_MODEL, 1), lambda j: (0, 0)),
            pl.BlockSpec((D_MODEL, D_MODEL), lambda j: (0, j)),
            pl.BlockSpec((1, D_MODEL), lambda j: (0, j)),
        ],
        out_specs=pl.BlockSpec((1, D_MODEL), lambda j: (0, j)),
        out_shape=jax.ShapeDtypeStruct((1, N_MOD * D_MODEL), F32),
        name="mod",
    )(c_col, w_ada, b_ada)


def _t5_bucket_table():
    L = ATTN_BLOCK
    dist = np.maximum((np.arange(L)[:, None] + L) - np.arange(2 * L)[None, :], 0)
    max_exact = N_REL_BUCKETS // 2
    nf = np.maximum(dist, 1).astype(np.float64)
    large = max_exact + (np.log(nf / max_exact) / math.log(REL_MAX_DISTANCE / max_exact)
                         * (N_REL_BUCKETS - max_exact)).astype(np.int32)
    large = np.minimum(large, N_REL_BUCKETS - 1)
    return np.where(dist < max_exact, dist, large).astype(np.int32)


def _bias_kernel(rb_ref, bucket_ref, o_ref):
    L = ATTN_BLOCK
    heads_per_kv = ATTN_Q_HEADS // ATTN_KV_HEADS
    bucket = bucket_ref[...]
    qi = lax.broadcasted_iota(jnp.int32, (L, 2 * L), 0) + L
    kj = lax.broadcasted_iota(jnp.int32, (L, 2 * L), 1)
    dist = qi - kj
    for h in range(ATTN_Q_HEADS):
        acc = jnp.zeros((L, 2 * L), F32)
        for b in range(N_REL_BUCKETS):
            acc = jnp.where(bucket == b, rb_ref[b, h], acc)
        band = jnp.where(dist >= 0, jnp.where(dist < WINDOW, acc, MASK_VALUE), MASK_VALUE)
        kvh, s, half = h // heads_per_kv, (h % heads_per_kv) // HEADS_PER_SLAB, h % HEADS_PER_SLAB
        rows = slice(s * L, (s + 1) * L)
        cols = slice(half * 2 * L, (half + 1) * 2 * L)
        o_ref[1, kvh, rows, cols] = band
        o_ref[0, kvh, rows, cols] = jnp.where(kj >= L, band, MASK_VALUE)


def _bias_table(rel_bias):
    L = ATTN_BLOCK
    shape = (2, ATTN_KV_HEADS, ATTN_Q_HEADS // ATTN_KV_HEADS // HEADS_PER_SLAB * L,
             HEADS_PER_SLAB * 2 * L)
    return pl.pallas_call(
        _bias_kernel,
        in_specs=[
            pl.BlockSpec(memory_space=pltpu.SMEM),
            pl.BlockSpec((L, 2 * L), lambda: (0, 0)),
        ],
        out_specs=pl.BlockSpec(shape, lambda: (0, 0, 0, 0)),
        out_shape=jax.ShapeDtypeStruct(shape, F32),
        name="bias",
    )(rel_bias, jnp.asarray(_t5_bucket_table()))


def _dup_heads(t):
    lo = lax.broadcasted_iota(jnp.int32, t.shape, 1) < HEAD_DIM
    swapped = pltpu.roll(t, HEAD_DIM, axis=1)
    return jnp.where(lo, t, swapped), jnp.where(lo, swapped, t)


def _inproj_kernel(x_ref, nw_ref, sc_ref, sh_ref, wq_ref, wkv_ref, wxbc_ref, wz_ref, wdt_ref,
                   q_ref, kv_ref, xbc_ref, z_ref, dt_ref):
    h = _rms(x_ref[...]) * nw_ref[...]
    h = (h * (1.0 + sc_ref[...]) + sh_ref[...]).astype(BF16)
    q_ref[...] = (_dot(h, wq_ref[...]) * (HEAD_DIM ** -0.5)).astype(BF16)
    kv = _dot(h, wkv_ref[...])
    k0, k1 = _dup_heads(kv[:, :KV_WIDTH])
    v0, v1 = _dup_heads(kv[:, KV_WIDTH:])
    kv_ref[:, 0 * LANES:1 * LANES] = k0.astype(BF16)
    kv_ref[:, 1 * LANES:2 * LANES] = k1.astype(BF16)
    kv_ref[:, 2 * LANES:3 * LANES] = v0.astype(BF16)
    kv_ref[:, 3 * LANES:4 * LANES] = v1.astype(BF16)
    xbc_ref[...] = _dot(h, wxbc_ref[...])
    z_ref[...] = _dot(h, wz_ref[...])
    dt_ref[...] = _dot(h, wdt_ref[...])


def _in_proj(x, nw, scale, shift, wq, wkv, wxbc, wz, wdt):
    tm = TOKEN_TILE
    row = lambda i: (i, 0)
    fixed = lambda i: (0, 0)
    vec = pl.BlockSpec((1, D_MODEL), fixed)
    full = lambda a: pl.BlockSpec(a.shape, fixed)
    return pl.pallas_call(
        _inproj_kernel,
        grid=(SEQ // tm,),
        in_specs=[pl.BlockSpec((tm, D_MODEL), row), vec, vec, vec,
                  full(wq), full(wkv), full(wxbc), full(wz), full(wdt)],
        out_specs=[
            pl.BlockSpec((tm, ATTN_WIDTH), row),
            pl.BlockSpec((tm, 4 * LANES), row),
            pl.BlockSpec((tm, XBC_WIDTH), row),
            pl.BlockSpec((tm, SSM_WIDTH), row),
            pl.BlockSpec((tm, LANES), row),
        ],
        out_shape=[
            jax.ShapeDtypeStruct((SEQ, ATTN_WIDTH), BF16),
            jax.ShapeDtypeStruct((SEQ, 4 * LANES), BF16),
            jax.ShapeDtypeStruct((SEQ, XBC_WIDTH), F32),
            jax.ShapeDtypeStruct((SEQ, SSM_WIDTH), F32),
            jax.ShapeDtypeStruct((SEQ, LANES), F32),
        ],
        compiler_params=pltpu.CompilerParams(
            dimension_semantics=("parallel",), vmem_limit_bytes=VMEM_LIMIT_BYTES),
        name="in_proj",
    )(x, nw, scale, shift, wq, wkv, wxbc, wz, wdt)


def _attn_kernel(sink_ref, q_ref, kv_ref, kvprev_ref, bias_ref, o_ref, band_ref):
    L = ATTN_BLOCK
    i = pl.program_id(0)
    band_ref[0:L, :] = kvprev_ref[...]
    band_ref[L:, :] = kv_ref[...]
    lane_lo = lax.broadcasted_iota(jnp.int32, (2 * L, LANES), 1) < HEAD_DIM
    out_lo = lax.broadcasted_iota(jnp.int32, (L, LANES), 1) < HEAD_DIM
    zero = jnp.zeros((2 * L, LANES), BF16)
    slabs_per_kv = ATTN_Q_HEADS // ATTN_KV_HEADS // HEADS_PER_SLAB

    def block_diag(t):
        return jnp.concatenate([jnp.where(lane_lo, t, zero), jnp.where(lane_lo, zero, t)], axis=0)

    def scores(b, kvh):
        r0 = b * L
        k2 = block_diag(band_ref[r0:r0 + 2 * L, kvh * LANES:(kvh + 1) * LANES])
        q2 = jnp.concatenate(
            [q_ref[r0:r0 + L, (kvh * slabs_per_kv + s) * LANES:(kvh * slabs_per_kv + s + 1) * LANES]
             for s in range(slabs_per_kv)], axis=0)
        table = jnp.where(i == 0, 0, 1) if b == 0 else 1
        return _dot_nt(q2, k2) + bias_ref[table, kvh]

    def softmax(kvh, sc):
        rows, scales = [], []
        for s in range(slabs_per_kv):
            ps, inv = [], []
            for half in range(HEADS_PER_SLAB):
                sink = sink_ref[0, (kvh * slabs_per_kv + s) * HEADS_PER_SLAB + half]
                sh = sc[s * L:(s + 1) * L, half * 2 * L:(half + 1) * 2 * L]
                m = jnp.maximum(jnp.max(sh, axis=-1, keepdims=True), sink)
                p = jnp.exp(sh - m)
                denom = jnp.sum(p, axis=-1, keepdims=True) + jnp.exp(sink - m)
                ps.append(p.astype(BF16))
                inv.append(1.0 / denom)
            rows.append(jnp.concatenate(ps, axis=1))
            scales.append(jnp.where(out_lo, inv[0], inv[1]))
        return jnp.concatenate(rows, axis=0), jnp.concatenate(scales, axis=0)

    def weighted_values(b, kvh, p, scale):
        r0 = b * L
        v2 = block_diag(band_ref[r0:r0 + 2 * L, (ATTN_KV_HEADS + kvh) * LANES:
                                 (ATTN_KV_HEADS + kvh + 1) * LANES])
        o = (_dot(p, v2) * scale).astype(o_ref.dtype)
        for s in range(slabs_per_kv):
            slab = kvh * slabs_per_kv + s
            o_ref[r0:r0 + L, slab * LANES:(slab + 1) * LANES] = o[s * L:(s + 1) * L, :]

    units = [(b, kvh) for b in range(TOKEN_TILE // L) for kvh in range(ATTN_KV_HEADS)]
    nxt = scores(*units[0])
    for n, (b, kvh) in enumerate(units):
        cur = nxt
        if n + 1 < len(units):
            nxt = scores(*units[n + 1])
        p, scale = softmax(kvh, cur)
        weighted_values(b, kvh, p, scale)


def _attention(sinks, q, kv, bias):
    tm, L = TOKEN_TILE, ATTN_BLOCK
    blocks_per_tile = tm // L
    row = lambda i: (i, 0)
    return pl.pallas_call(
        _attn_kernel,
        grid=(SEQ // tm,),
        in_specs=[
            pl.BlockSpec(memory_space=pltpu.SMEM),
            pl.BlockSpec((tm, ATTN_WIDTH), row),
            pl.BlockSpec((tm, 4 * LANES), row),
            pl.BlockSpec((L, 4 * LANES), lambda i: (jnp.maximum(i * blocks_per_tile - 1, 0), 0)),
            pl.BlockSpec(bias.shape, lambda i: (0, 0, 0, 0)),
        ],
        out_specs=pl.BlockSpec((tm, ATTN_WIDTH), row),
        out_shape=jax.ShapeDtypeStruct((SEQ, ATTN_WIDTH), BF16),
        scratch_shapes=[pltpu.VMEM((L + tm, 4 * LANES), BF16)],
        compiler_params=pltpu.CompilerParams(
            dimension_semantics=("parallel",), vmem_limit_bytes=VMEM_LIMIT_BYTES),
        name="attn",
    )(sinks, q, kv, kv, bias)


def _ssd_kernel(xbc_ref, halo_ref, z_ref, dt_ref, cw_ref, cb_ref, dtb_ref, alog_ref, dskip_ref,
                nw_ref, o_ref, state_ref):
    T = SSM_CHUNK
    half_t = T // 2
    c = pl.program_id(0)

    @pl.when(c == 0)
    def _():
        state_ref[...] = jnp.zeros_like(state_ref)

    xbc = xbc_ref[...]
    padded = jnp.concatenate([jnp.where(c > 0, halo_ref[...], 0.0), xbc], axis=0)
    acc = cb_ref[...] + cw_ref[SSM_CONV - 1:SSM_CONV, :] * xbc
    for k in range(SSM_CONV - 1):
        shifted = pltpu.roll(padded, SSM_CONV - 1 - k, axis=0)[SUBLANES:, :]
        acc = acc + cw_ref[k:k + 1, :] * shifted
    act = jax.nn.silu(acc)
    xs = act[:, :SSM_WIDTH]

    dt = jax.nn.softplus(dt_ref[...] + dtb_ref[...])
    adt = dt * (-jnp.exp(alog_ref[...]))
    ti = lax.broadcasted_iota(jnp.int32, (T, T), 0)
    tj = lax.broadcasted_iota(jnp.int32, (T, T), 1)
    acum = _dot_select(jnp.where(ti >= tj, 1.0, 0.0).astype(BF16), adt)
    acum_t = acum.T
    causal_half = (lax.broadcasted_iota(jnp.int32, (half_t, half_t), 0)
                   >= lax.broadcasted_iota(jnp.int32, (half_t, half_t), 1))

    er = lax.broadcasted_iota(jnp.int32, (LANES, SSM_WIDTH), 0)
    ec = lax.broadcasted_iota(jnp.int32, (LANES, SSM_WIDTH), 1)
    expand = jnp.where(jnp.logical_and(ec >= er * SSM_HEAD_DIM, ec < (er + 1) * SSM_HEAD_DIM),
                       1.0, 0.0).astype(BF16)
    acum_e = _select_dot(acum, expand)
    dt_e = _select_dot(dt, expand)
    last_e = acum_e[T - 1:T, :]
    from_start = jnp.exp(acum_e)
    to_end = jnp.exp(last_e - acum_e)
    chunk_decay = jnp.exp(last_e)

    xc = xs * dt_e
    xc_b = xc.astype(BF16)
    xce_b = (xc * to_end).astype(BF16)
    lane_lo = lax.broadcasted_iota(jnp.int32, (half_t, LANES), 1) < SSM_HEAD_DIM
    heads_per_group = SSM_HEADS // SSM_GROUPS
    gw = heads_per_group * SSM_HEAD_DIM

    y_parts = []
    for g in range(SSM_GROUPS):
        b_g = act[:, SSM_WIDTH + g * SSM_STATE:SSM_WIDTH + (g + 1) * SSM_STATE].astype(BF16)
        c_g = act[:, SSM_WIDTH + SSM_BC + g * SSM_STATE:
                  SSM_WIDTH + SSM_BC + (g + 1) * SSM_STATE].astype(BF16)
        gcols = slice(g * gw, (g + 1) * gw)
        y_off = _dot(c_g, state_ref[:, gcols].astype(BF16)) * from_start[:, gcols]
        new_states = _dot_tn(b_g, xce_b[:, gcols])
        state_ref[:, gcols] = state_ref[:, gcols] * chunk_decay[:, gcols] + new_states
        y_rows = []
        for r in range(2):
            rows = slice(r * half_t, (r + 1) * half_t)
            n_keys = (r + 1) * half_t
            cbm = _dot_nt(c_g[rows, :], b_g[:n_keys, :])
            slabs = []
            for s in range(heads_per_group // HEADS_PER_SLAB):
                slab = g * (heads_per_group // HEADS_PER_SLAB) + s
                x_slab = xc_b[:n_keys, slab * LANES:(slab + 1) * LANES]
                halves = []
                for half in range(HEADS_PER_SLAB):
                    h = slab * HEADS_PER_SLAB + half
                    diff = acum[rows, h:h + 1] - acum_t[h:h + 1, :n_keys]
                    decay = jnp.exp(jnp.where(causal_half, diff[:, r * half_t:], MASK_VALUE))
                    if r:
                        decay = jnp.concatenate([jnp.exp(diff[:, :r * half_t]), decay], axis=1)
                    halves.append(_dot((cbm * decay).astype(BF16), x_slab))
                slabs.append(jnp.where(lane_lo, halves[0], halves[1]))
            y_rows.append(jnp.concatenate(slabs, axis=1))
        y_parts.append(jnp.concatenate(y_rows, axis=0) + y_off)
    y = jnp.concatenate(y_parts, axis=1) + dskip_ref[...] * xs
    y = y * jax.nn.silu(z_ref[...])
    norm_w = nw_ref[...]
    gn = SSM_WIDTH // SSM_GROUPS
    for g in range(SSM_GROUPS):
        cols = slice(g * gn, (g + 1) * gn)
        o_ref[:, cols] = (_rms(y[:, cols]) * norm_w[:, cols]).astype(o_ref.dtype)


def _ssd(xbc, z, dt_raw, conv_w, conv_b, dt_bias, a_log, d_skip, norm_w):
    T = SSM_CHUNK
    row = lambda c: (c, 0)
    fixed = lambda c: (0, 0)
    full = lambda a: pl.BlockSpec(a.shape, fixed)
    return pl.pallas_call(
        _ssd_kernel,
        grid=(SEQ // T,),
        in_specs=[
            pl.BlockSpec((T, XBC_WIDTH), row),
            pl.BlockSpec((SUBLANES, XBC_WIDTH), lambda c: (jnp.maximum(c * (T // SUBLANES) - 1, 0), 0)),
            pl.BlockSpec((T, SSM_WIDTH), row),
            pl.BlockSpec((T, LANES), row),
            full(conv_w), full(conv_b), full(dt_bias), full(a_log), full(d_skip), full(norm_w),
        ],
        out_specs=pl.BlockSpec((T, SSM_WIDTH), row),
        out_shape=jax.ShapeDtypeStruct((SEQ, SSM_WIDTH), BF16),
        scratch_shapes=[pltpu.VMEM((SSM_STATE, SSM_WIDTH), F32)],
        compiler_params=pltpu.CompilerParams(
            dimension_semantics=("arbitrary",), vmem_limit_bytes=VMEM_LIMIT_BYTES),
        name="ssd",
    )(xbc, xbc, z, dt_raw, conv_w, conv_b, dt_bias, a_log, d_skip, norm_w)


def _ffn_kernel(x_ref, a_ref, s_ref, g1_ref, pmw_ref, pfw_ref, sc2_ref, sh2_ref, g2_ref, pow_ref,
                wout_ref, wup_ref, cw_ref, cb_ref, wdown_ref, o_ref,
                h_ref, acc_ref, stage_ref, carry_ref):
    tm = TOKEN_TILE
    i = pl.program_id(0)
    n_slabs = D_MODEL // LANES

    @pl.when(i == 0)
    def _():
        carry_ref[...] = jnp.zeros_like(carry_ref)

    mixed = _dot(a_ref[...], wout_ref[:ATTN_WIDTH, :]) + _dot(s_ref[...], wout_ref[ATTN_WIDTH:, :])
    x1 = x_ref[...] + g1_ref[...] * (_rms(mixed) * pmw_ref[...])
    o_ref[...] = x1
    h = _rms(x1) * pfw_ref[...]
    h = h * (1.0 + sc2_ref[...]) + sh2_ref[...]
    for l in range(n_slabs):
        for s in range(SUBLANES):
            stage_ref[l, s * SEG_PITCH:s * SEG_PITCH + SEG, :] = (
                h[s * SEG:(s + 1) * SEG, l * LANES:(l + 1) * LANES])
    for l in range(n_slabs):
        rows = [stage_ref[l, pl.ds(v, SUBLANES, stride=SEG_PITCH), :] for v in range(SEG)]
        h_ref[:, l * LANES:(l + 1) * LANES] = jnp.concatenate(rows, axis=0).astype(BF16)
    acc_ref[...] = jnp.zeros_like(acc_ref)

    last_sublane = lax.broadcasted_iota(jnp.int32, (SUBLANES, FF_BLOCK), 0) == SUBLANES - 1

    def conv(u, cols):
        tail = u[tm - 2 * SUBLANES:, :]
        prev_tail = carry_ref[:, cols]
        carry_ref[:, cols] = tail
        before = [pltpu.roll(jnp.where(last_sublane, prev_tail[r * SUBLANES:(r + 1) * SUBLANES, :],
                                       tail[r * SUBLANES:(r + 1) * SUBLANES, :]), 1, axis=0)
                  for r in range(FFN_CONV - 1)]
        u_m1 = jnp.concatenate([before[1], u[:tm - SUBLANES, :]], axis=0)
        u_m2 = jnp.concatenate([before[0], before[1], u[:tm - 2 * SUBLANES, :]], axis=0)
        return (cb_ref[:, cols] + cw_ref[2:3, cols] * u + cw_ref[1:2, cols] * u_m1
                + cw_ref[0:1, cols] * u_m2)

    n_blocks = D_FF // FF_BLOCK
    gate_cols = lambda j: slice(j * FF_BLOCK, (j + 1) * FF_BLOCK)
    val_cols = lambda j: slice(D_FF + j * FF_BLOCK, D_FF + (j + 1) * FF_BLOCK)

    def up(j):
        hb = h_ref[...]
        return _dot(hb, wup_ref[:, gate_cols(j)]), _dot(hb, wup_ref[:, val_cols(j)])

    def down(j, f):
        acc_ref[...] += _dot(f, wdown_ref[gate_cols(j), :])

    nxt = up(0)
    f_prev = None
    for j in range(n_blocks):
        cur = nxt
        if j + 1 < n_blocks:
            nxt = up(j + 1)
        if f_prev is not None:
            down(j - 1, f_prev)
        u_gate = conv(cur[0], gate_cols(j))
        u_val = conv(cur[1], val_cols(j))
        f_prev = (jax.nn.gelu(u_gate, approximate=True) * u_val).astype(BF16)
    down(n_blocks - 1, f_prev)

    y = g2_ref[...] * (_rms(acc_ref[...]) * pow_ref[...])
    for l in range(n_slabs):
        stage_ref[l, 0:tm, :] = y[:, l * LANES:(l + 1) * LANES]
    for l in range(n_slabs):
        rows = [stage_ref[l, pl.ds(k // SUBLANES + SEG * (k % SUBLANES), SUBLANES, stride=SUBLANES), :]
                for k in range(SEG)]
        o_ref[:, l * LANES:(l + 1) * LANES] += jnp.concatenate(rows, axis=0)


def _out_ffn(x, attn, ssm, g1, pmw, pfw, sc2, sh2, g2, pow_, w_out, w_up, cw, cb, w_down):
    tm = TOKEN_TILE
    row = lambda i: (i, 0)
    fixed = lambda i: (0, 0)
    vec = pl.BlockSpec((1, D_MODEL), fixed)
    resident = lambda a: pl.BlockSpec(a.shape, fixed, pipeline_mode=pl.Buffered(1))
    return pl.pallas_call(
        _ffn_kernel,
        grid=(SEQ // tm,),
        in_specs=[
            pl.BlockSpec((tm, D_MODEL), row),
            pl.BlockSpec((tm, ATTN_WIDTH), row),
            pl.BlockSpec((tm, SSM_WIDTH), row),
            vec, vec, vec, vec, vec, vec, vec,
            resident(w_out), resident(w_up), resident(cw), resident(cb), resident(w_down),
        ],
        out_specs=pl.BlockSpec((tm, D_MODEL), row),
        out_shape=jax.ShapeDtypeStruct((SEQ, D_MODEL), F32),
        scratch_shapes=[
            pltpu.VMEM((tm, D_MODEL), BF16),
            pltpu.VMEM((tm, D_MODEL), F32),
            pltpu.VMEM((D_MODEL // LANES, SUBLANES * SEG_PITCH, LANES), F32),
            pltpu.VMEM(((FFN_CONV - 1) * SUBLANES, 2 * D_FF), F32),
        ],
        compiler_params=pltpu.CompilerParams(
            dimension_semantics=("arbitrary",), vmem_limit_bytes=VMEM_LIMIT_BYTES),
        name="ffn",
    )(x, attn, ssm, g1, pmw, pfw, sc2, sh2, g2, pow_, w_out, w_up, cw, cb, w_down)


def _pad_lanes(a):
    return jnp.pad(a, ((0, 0), (0, LANES - a.shape[-1])))


def kernel(x, c, rel_bias, w_ada, b_ada, pre_mix_w, w_in, attn_sinks, ssm_conv_w, ssm_conv_b,
           ssm_dt_bias, ssm_a_log, ssm_d, ssm_norm_w, w_out, post_mix_w, pre_ffn_w, w_up,
           ffn_conv_w, ffn_conv_b, w_down, post_ffn_w):
    bsz, s, d = x.shape
    assert (bsz, s, d) == (1, SEQ, D_MODEL) and w_ada.shape[0] == 1
    x2 = x.reshape(s, d)

    mod = _modulation(c.reshape(d, 1), w_ada[0], b_ada[0].reshape(1, -1))
    shift1, scale1, gate1, shift2, scale2, gate2 = jnp.split(mod, N_MOD, axis=-1)
    bias = _bias_table(rel_bias)

    o_q, o_k, o_v, o_xs, o_z, o_b, o_c, o_dt = np.cumsum(
        (0, ATTN_WIDTH, KV_WIDTH, KV_WIDTH, SSM_WIDTH, SSM_WIDTH, SSM_BC, SSM_BC)).tolist()
    wi = w_in[0]
    wq = wi[:, o_q:o_k].astype(BF16)
    wkv = wi[:, o_k:o_xs].astype(BF16)
    wxbc = jnp.concatenate([wi[:, o_xs:o_z], wi[:, o_b:o_dt]], axis=1).astype(BF16)
    wz = wi[:, o_z:o_b].astype(BF16)
    wdt = _pad_lanes(wi[:, o_dt:]).astype(BF16)
    q, kv, xbc, z, dt_raw = _in_proj(x2, pre_mix_w, scale1, shift1, wq, wkv, wxbc, wz, wdt)

    attn = _attention(attn_sinks, q, kv, bias)
    ssm = _ssd(xbc, z, dt_raw, ssm_conv_w[0], ssm_conv_b, _pad_lanes(ssm_dt_bias),
               _pad_lanes(ssm_a_log), jnp.repeat(ssm_d, SSM_HEAD_DIM, axis=1), ssm_norm_w)

    out = _out_ffn(x2, attn, ssm, gate1, post_mix_w, pre_ffn_w, scale2, shift2, gate2, post_ffn_w,
                   w_out[0].astype(BF16), w_up[0].astype(BF16), ffn_conv_w[0], ffn_conv_b,
                   w_down[0].astype(BF16))
    return out.reshape(bsz, s, d)
```

```python
import math

import numpy as np
import jax
import jax.numpy as jnp
from jax import lax
from jax.experimental import pallas as pl
from jax.experimental.pallas import tpu as pltpu

D_MODEL = 1024
SEQ = 16384
N_MOD = 6

ATTN_Q_HEADS = 8
ATTN_KV_HEADS = 2
HEAD_DIM = 64
ATTN_WIDTH = ATTN_Q_HEADS * HEAD_DIM
KV_WIDTH = ATTN_KV_HEADS * HEAD_DIM
WINDOW = 128
ATTN_BLOCK = 128
N_REL_BUCKETS = 32
REL_MAX_DISTANCE = 128

SSM_HEADS = 8
SSM_HEAD_DIM = 64
SSM_WIDTH = SSM_HEADS * SSM_HEAD_DIM
SSM_STATE = 128
SSM_GROUPS = 2
SSM_BC = SSM_GROUPS * SSM_STATE
SSM_CONV = 4
SSM_CHUNK = 256
XBC_WIDTH = SSM_WIDTH + 2 * SSM_BC

D_FF = 2816
FFN_CONV = 3
NORM_EPS = 1e-6
MASK_VALUE = -1e30

LANES = 128
SUBLANES = 8
VMEM_LIMIT_BYTES = 56 * 1024 * 1024

TOKEN_TILE = 512
FF_BLOCK = 256
DOWN_GROUP = 2
HEADS_PER_SLAB = LANES // HEAD_DIM
SEG = TOKEN_TILE // SUBLANES
SEG_PITCH = SEG + SUBLANES

BF16 = jnp.bfloat16
F32 = jnp.float32


def _dot(a, b):
    return jnp.dot(a, b, preferred_element_type=F32)


def _dot_nt(a, b):
    return lax.dot_general(a, b, (((1,), (1,)), ((), ())), preferred_element_type=F32)


def _dot_tn(a, b):
    return lax.dot_general(a, b, (((0,), (0,)), ((), ())), preferred_element_type=F32)


def _split3(x):
    x1 = x.astype(BF16)
    r1 = x - x1.astype(F32)
    x2 = r1.astype(BF16)
    r2 = r1 - x2.astype(F32)
    return x1, x2, r2.astype(BF16)


def _dot_select(sel, x):
    x1, x2, x3 = _split3(x)
    return _dot(sel, x1) + _dot(sel, x2) + _dot(sel, x3)


def _select_dot(x, sel):
    x1, x2, x3 = _split3(x)
    return _dot(x1, sel) + _dot(x2, sel) + _dot(x3, sel)


def _rms(x):
    return x * lax.rsqrt(jnp.mean(x * x, axis=-1, keepdims=True) + NORM_EPS)


def _mod_kernel(c_ref, w_ref, b_ref, o_ref):
    cond = jax.nn.silu(c_ref[...])
    o_ref[...] = jnp.sum(cond * w_ref[...], axis=0, keepdims=True) + b_ref[...]


def _modulation(c_col, w_ada, b_ada):
    return pl.pallas_call(
        _mod_kernel,
        grid=(N_MOD,),
        in_specs=[
            pl.BlockSpec((D_MODEL, 1), lambda j: (0, 0)),
            pl.BlockSpec((D_MODEL, D_MODEL), lambda j: (0, j)),
            pl.BlockSpec((1, D_MODEL), lambda j: (0, j)),
        ],
        out_specs=pl.BlockSpec((1, D_MODEL), lambda j: (0, j)),
        out_shape=jax.ShapeDtypeStruct((1, N_MOD * D_MODEL), F32),
        name="mod",
    )(c_col, w_ada, b_ada)


def _t5_bucket_table():
    L = ATTN_BLOCK
    dist = np.maximum((np.arange(L)[:, None] + L) - np.arange(2 * L)[None, :], 0)
    max_exact = N_REL_BUCKETS // 2
    nf = np.maximum(dist, 1).astype(np.float64)
    large = max_exact + (np.log(nf / max_exact) / math.log(REL_MAX_DISTANCE / max_exact)
                         * (N_REL_BUCKETS - max_exact)).astype(np.int32)
    large = np.minimum(large, N_REL_BUCKETS - 1)
    return np.where(dist < max_exact, dist, large).astype(np.int32)


def _bias_kernel(rb_ref, bucket_ref, o_ref):
    L = ATTN_BLOCK
    heads_per_kv = ATTN_Q_HEADS // ATTN_KV_HEADS
    bucket = bucket_ref[...]
    qi = lax.broadcasted_iota(jnp.int32, (L, 2 * L), 0) + L
    kj = lax.broadcasted_iota(jnp.int32, (L, 2 * L), 1)
    dist = qi - kj
    for h in range(ATTN_Q_HEADS):
        acc = jnp.zeros((L, 2 * L), F32)
        for b in range(N_REL_BUCKETS):
            acc = jnp.where(bucket == b, rb_ref[b, h], acc)
        band = jnp.where(dist >= 0, jnp.where(dist < WINDOW, acc, MASK_VALUE), MASK_VALUE)
        kvh, s, half = h // heads_per_kv, (h % heads_per_kv) // HEADS_PER_SLAB, h % HEADS_PER_SLAB
        rows = slice(s * L, (s + 1) * L)
        cols = slice(half * 2 * L, (half + 1) * 2 * L)
        o_ref[1, kvh, rows, cols] = band
        o_ref[0, kvh, rows, cols] = jnp.where(kj >= L, band, MASK_VALUE)


def _bias_table(rel_bias):
    L = ATTN_BLOCK
    shape = (2, ATTN_KV_HEADS, ATTN_Q_HEADS // ATTN_KV_HEADS // HEADS_PER_SLAB * L,
             HEADS_PER_SLAB * 2 * L)
    return pl.pallas_call(
        _bias_kernel,
        in_specs=[
            pl.BlockSpec(memory_space=pltpu.SMEM),
            pl.BlockSpec((L, 2 * L), lambda: (0, 0)),
        ],
        out_specs=pl.BlockSpec(shape, lambda: (0, 0, 0, 0)),
        out_shape=jax.ShapeDtypeStruct(shape, F32),
        name="bias",
    )(rel_bias, jnp.asarray(_t5_bucket_table()))


def _dup_heads(t):
    lo = lax.broadcasted_iota(jnp.int32, t.shape, 1) < HEAD_DIM
    swapped = pltpu.roll(t, HEAD_DIM, axis=1)
    return jnp.where(lo, t, swapped), jnp.where(lo, swapped, t)


_IN_Q, _IN_K, _IN_V, _IN_XS, _IN_Z, _IN_B, _IN_C, _IN_DT, IN_PROJ_WIDTH = np.cumsum(
    (0, ATTN_WIDTH, KV_WIDTH, KV_WIDTH, SSM_WIDTH, SSM_WIDTH, SSM_BC, SSM_BC, SSM_HEADS)).tolist()
_W_Q, _W_KV, _W_XBC, _W_Z, _W_DT, _W_END = np.cumsum(
    (0, ATTN_WIDTH, 2 * KV_WIDTH, XBC_WIDTH, SSM_WIDTH, LANES)).tolist()
CAST_ROWS = D_MODEL // (SEQ // TOKEN_TILE)


def _inproj_kernel(x_ref, nw_ref, sc_ref, sh_ref, win_ref, cw_ref, cb_ref, dtb_ref,
                   wout_ref, wup_ref, wdown_ref,
                   q_ref, kv_ref, xs_ref, bc_ref, zs_ref, dt_ref, wout_b_ref, wup_b_ref, wdown_b_ref,
                   w_ref, carry_ref):
    i = pl.program_id(0)

    @pl.when(i == 0)
    def _():
        w_ref[:, _W_Q:_W_KV] = win_ref[:, _IN_Q:_IN_K].astype(BF16)
        w_ref[:, _W_KV:_W_XBC] = win_ref[:, _IN_K:_IN_XS].astype(BF16)
        w_ref[:, _W_XBC:_W_XBC + SSM_WIDTH] = win_ref[:, _IN_XS:_IN_Z].astype(BF16)
        w_ref[:, _W_XBC + SSM_WIDTH:_W_Z] = win_ref[:, _IN_B:_IN_DT].astype(BF16)
        w_ref[:, _W_Z:_W_DT] = win_ref[:, _IN_Z:_IN_B].astype(BF16)
        w_ref[:, _W_DT:_W_END] = jnp.zeros((D_MODEL, LANES), BF16)
        w_ref[:, _W_DT:_W_DT + SSM_HEADS] = win_ref[:, _IN_DT:IN_PROJ_WIDTH].astype(BF16)
        carry_ref[...] = jnp.zeros_like(carry_ref)

    h = _rms(x_ref[...]) * nw_ref[...]
    h = (h * (1.0 + sc_ref[...]) + sh_ref[...]).astype(BF16)

    def conv_silu(u, cols):
        padded = jnp.concatenate([carry_ref[:, cols], u], axis=0)
        carry_ref[:, cols] = u[TOKEN_TILE - SUBLANES:, :]
        acc = cb_ref[:, cols] + cw_ref[SSM_CONV - 1:SSM_CONV, cols] * u
        for k in range(SSM_CONV - 1):
            shifted = pltpu.roll(padded, SSM_CONV - 1 - k, axis=0)[SUBLANES:, :]
            acc = acc + cw_ref[k:k + 1, cols] * shifted
        return jax.nn.silu(acc)

    u_xs = _dot(h, w_ref[:, _W_XBC:_W_XBC + SSM_WIDTH])
    u_bc = _dot(h, w_ref[:, _W_XBC + SSM_WIDTH:_W_Z])
    u_z = _dot(h, w_ref[:, _W_Z:_W_DT])
    u_q = _dot(h, w_ref[:, _W_Q:_W_KV])
    kv = _dot(h, w_ref[:, _W_KV:_W_XBC])
    u_dt = _dot(h, w_ref[:, _W_DT:_W_END])
    wout_b_ref[...] = wout_ref[...].astype(BF16)
    wup_b_ref[...] = wup_ref[...].astype(BF16)
    wdown_b_ref[...] = wdown_ref[...].astype(BF16)
    xs_ref[...] = conv_silu(u_xs, slice(0, SSM_WIDTH))
    bc_ref[...] = conv_silu(u_bc, slice(SSM_WIDTH, XBC_WIDTH)).astype(BF16)
    zs_ref[...] = jax.nn.silu(u_z)
    q_ref[...] = (u_q * (HEAD_DIM ** -0.5)).astype(BF16)
    k0, k1 = _dup_heads(kv[:, :KV_WIDTH])
    v0, v1 = _dup_heads(kv[:, KV_WIDTH:])
    kv_ref[:, 0 * LANES:1 * LANES] = k0.astype(BF16)
    kv_ref[:, 1 * LANES:2 * LANES] = k1.astype(BF16)
    kv_ref[:, 2 * LANES:3 * LANES] = v0.astype(BF16)
    kv_ref[:, 3 * LANES:4 * LANES] = v1.astype(BF16)
    dt_ref[...] = jax.nn.softplus(u_dt + dtb_ref[...])


def _in_proj(x, nw, scale, shift, w_in, conv_w, conv_b, dt_bias, w_out, w_up, w_down2):
    tm = TOKEN_TILE
    row = lambda i: (i, 0)
    fixed = lambda i: (0, 0)
    vec = pl.BlockSpec((1, D_MODEL), fixed)
    full = lambda a: pl.BlockSpec(a.shape, fixed)
    cast_rows = lambda a: pl.BlockSpec((CAST_ROWS, a.shape[1]), row)
    return pl.pallas_call(
        _inproj_kernel,
        grid=(SEQ // tm,),
        in_specs=[pl.BlockSpec((tm, D_MODEL), row), vec, vec, vec,
                  pl.BlockSpec(w_in.shape, fixed, pipeline_mode=pl.Buffered(1)),
                  full(conv_w), full(conv_b), full(dt_bias),
                  cast_rows(w_out), cast_rows(w_up), cast_rows(w_down2)],
        out_specs=[
            pl.BlockSpec((tm, ATTN_WIDTH), row),
            pl.BlockSpec((tm, 4 * LANES), row),
            pl.BlockSpec((tm, SSM_WIDTH), row),
            pl.BlockSpec((tm, 2 * SSM_BC), row),
            pl.BlockSpec((tm, SSM_WIDTH), row),
            pl.BlockSpec((tm, LANES), row),
            cast_rows(w_out), cast_rows(w_up), cast_rows(w_down2),
        ],
        out_shape=[
            jax.ShapeDtypeStruct((SEQ, ATTN_WIDTH), BF16),
            jax.ShapeDtypeStruct((SEQ, 4 * LANES), BF16),
            jax.ShapeDtypeStruct((SEQ, SSM_WIDTH), F32),
            jax.ShapeDtypeStruct((SEQ, 2 * SSM_BC), BF16),
            jax.ShapeDtypeStruct((SEQ, SSM_WIDTH), F32),
            jax.ShapeDtypeStruct((SEQ, LANES), F32),
            jax.ShapeDtypeStruct(w_out.shape, BF16),
            jax.ShapeDtypeStruct(w_up.shape, BF16),
            jax.ShapeDtypeStruct(w_down2.shape, BF16),
        ],
        scratch_shapes=[
            pltpu.VMEM((D_MODEL, _W_END), BF16),
            pltpu.VMEM((SUBLANES, XBC_WIDTH), F32),
        ],
        compiler_params=pltpu.CompilerParams(
            dimension_semantics=("arbitrary",), vmem_limit_bytes=VMEM_LIMIT_BYTES),
        name="in_proj",
    )(x, nw, scale, shift, w_in, conv_w, conv_b, dt_bias, w_out, w_up, w_down2)


def _attn_kernel(sink_ref, q_ref, kv_ref, kvprev_ref, bias_ref, o_ref, band_ref):
    L = ATTN_BLOCK
    i = pl.program_id(0)
    band_ref[0:L, :] = kvprev_ref[...]
    band_ref[L:, :] = kv_ref[...]
    lane_lo = lax.broadcasted_iota(jnp.int32, (2 * L, LANES), 1) < HEAD_DIM
    out_lo = lax.broadcasted_iota(jnp.int32, (L, LANES), 1) < HEAD_DIM
    zero = jnp.zeros((2 * L, LANES), BF16)
    slabs_per_kv = ATTN_Q_HEADS // ATTN_KV_HEADS // HEADS_PER_SLAB

    def block_diag(t):
        return jnp.concatenate([jnp.where(lane_lo, t, zero), jnp.where(lane_lo, zero, t)], axis=0)

    def scores(b, kvh):
        r0 = b * L
        k2 = block_diag(band_ref[r0:r0 + 2 * L, kvh * LANES:(kvh + 1) * LANES])
        q2 = jnp.concatenate(
            [q_ref[r0:r0 + L, (kvh * slabs_per_kv + s) * LANES:(kvh * slabs_per_kv + s + 1) * LANES]
             for s in range(slabs_per_kv)], axis=0)
        table = jnp.where(i == 0, 0, 1) if b == 0 else 1
        return _dot_nt(q2, k2) + bias_ref[table, kvh]

    def softmax(kvh, sc):
        rows, scales = [], []
        for s in range(slabs_per_kv):
            ps, inv = [], []
            for half in range(HEADS_PER_SLAB):
                sink = sink_ref[0, (kvh * slabs_per_kv + s) * HEADS_PER_SLAB + half]
                sh = sc[s * L:(s + 1) * L, half * 2 * L:(half + 1) * 2 * L]
                m = jnp.maximum(jnp.max(sh, axis=-1, keepdims=True), sink)
                p = jnp.exp(sh - m)
                denom = jnp.sum(p, axis=-1, keepdims=True) + jnp.exp(sink - m)
                ps.append(p.astype(BF16))
                inv.append(1.0 / denom)
            rows.append(jnp.concatenate(ps, axis=1))
            scales.append(jnp.where(out_lo, inv[0], inv[1]))
        return jnp.concatenate(rows, axis=0), jnp.concatenate(scales, axis=0)

    def weighted_values(b, kvh, p, scale):
        r0 = b * L
        v2 = block_diag(band_ref[r0:r0 + 2 * L, (ATTN_KV_HEADS + kvh) * LANES:
                                 (ATTN_KV_HEADS + kvh + 1) * LANES])
        o = (_dot(p, v2) * scale).astype(o_ref.dtype)
        for s in range(slabs_per_kv):
            slab = kvh * slabs_per_kv + s
            o_ref[r0:r0 + L, slab * LANES:(slab + 1) * LANES] = o[s * L:(s + 1) * L, :]

    units = [(b, kvh) for b in range(TOKEN_TILE // L) for kvh in range(ATTN_KV_HEADS)]
    nxt = scores(*units[0])
    for n, (b, kvh) in enumerate(units):
        cur = nxt
        if n + 1 < len(units):
            nxt = scores(*units[n + 1])
        p, scale = softmax(kvh, cur)
        weighted_values(b, kvh, p, scale)


def _attention(sinks, q, kv, bias):
    tm, L = TOKEN_TILE, ATTN_BLOCK
    blocks_per_tile = tm // L
    row = lambda i: (i, 0)
    return pl.pallas_call(
        _attn_kernel,
        grid=(SEQ // tm,),
        in_specs=[
            pl.BlockSpec(memory_space=pltpu.SMEM),
            pl.BlockSpec((tm, ATTN_WIDTH), row),
            pl.BlockSpec((tm, 4 * LANES), row),
            pl.BlockSpec((L, 4 * LANES), lambda i: (jnp.maximum(i * blocks_per_tile - 1, 0), 0)),
            pl.BlockSpec(bias.shape, lambda i: (0, 0, 0, 0)),
        ],
        out_specs=pl.BlockSpec((tm, ATTN_WIDTH), row),
        out_shape=jax.ShapeDtypeStruct((SEQ, ATTN_WIDTH), BF16),
        scratch_shapes=[pltpu.VMEM((L + tm, 4 * LANES), BF16)],
        compiler_params=pltpu.CompilerParams(
            dimension_semantics=("parallel",), vmem_limit_bytes=VMEM_LIMIT_BYTES),
        name="attn",
    )(sinks, q, kv, kv, bias)


def _ssd_kernel(xs_ref, bc_ref, zs_ref, dt_ref, alog_ref, dskip_ref, nw_ref, o_ref, state_ref):
    T = SSM_CHUNK
    half_t = T // 2
    c = pl.program_id(0)

    @pl.when(c == 0)
    def _():
        state_ref[...] = jnp.zeros_like(state_ref)

    xs = xs_ref[...]
    dt = dt_ref[...]
    adt = dt * (-jnp.exp(alog_ref[...]))
    ti = lax.broadcasted_iota(jnp.int32, (T, T), 0)
    tj = lax.broadcasted_iota(jnp.int32, (T, T), 1)
    acum = _dot_select(jnp.where(ti >= tj, 1.0, 0.0).astype(BF16), adt)
    acum_t = acum.T
    causal_half = (lax.broadcasted_iota(jnp.int32, (half_t, half_t), 0)
                   >= lax.broadcasted_iota(jnp.int32, (half_t, half_t), 1))

    er = lax.broadcasted_iota(jnp.int32, (LANES, SSM_WIDTH), 0)
    ec = lax.broadcasted_iota(jnp.int32, (LANES, SSM_WIDTH), 1)
    expand = jnp.where(jnp.logical_and(ec >= er * SSM_HEAD_DIM, ec < (er + 1) * SSM_HEAD_DIM),
                       1.0, 0.0).astype(BF16)
    acum_e = _select_dot(acum, expand)
    dt_e = _select_dot(dt, expand)
    last_e = acum_e[T - 1:T, :]
    from_start = jnp.exp(acum_e)
    to_end = jnp.exp(last_e - acum_e)
    chunk_decay = jnp.exp(last_e)

    xc = xs * dt_e
    xc_b = xc.astype(BF16)
    xce_b = (xc * to_end).astype(BF16)
    lane_lo = lax.broadcasted_iota(jnp.int32, (half_t, LANES), 1) < SSM_HEAD_DIM
    heads_per_group = SSM_HEADS // SSM_GROUPS
    gw = heads_per_group * SSM_HEAD_DIM

    y_parts = []
    for g in range(SSM_GROUPS):
        b_g = bc_ref[:, g * SSM_STATE:(g + 1) * SSM_STATE]
        c_g = bc_ref[:, SSM_BC + g * SSM_STATE:SSM_BC + (g + 1) * SSM_STATE]
        gcols = slice(g * gw, (g + 1) * gw)
        y_off = _dot(c_g, state_ref[:, gcols].astype(BF16)) * from_start[:, gcols]
        new_states = _dot_tn(b_g, xce_b[:, gcols])
        state_ref[:, gcols] = state_ref[:, gcols] * chunk_decay[:, gcols] + new_states
        y_rows = []
        for r in range(2):
            rows = slice(r * half_t, (r + 1) * half_t)
            n_keys = (r + 1) * half_t
            cbm = _dot_nt(c_g[rows, :], b_g[:n_keys, :])
            slabs = []
            for s in range(heads_per_group // HEADS_PER_SLAB):
                slab = g * (heads_per_group // HEADS_PER_SLAB) + s
                x_slab = xc_b[:n_keys, slab * LANES:(slab + 1) * LANES]
                halves = []
                for half in range(HEADS_PER_SLAB):
                    h = slab * HEADS_PER_SLAB + half
                    diff = acum[rows, h:h + 1] - acum_t[h:h + 1, :n_keys]
                    decay = jnp.exp(jnp.where(causal_half, diff[:, r * half_t:], MASK_VALUE))
                    if r:
                        decay = jnp.concatenate([jnp.exp(diff[:, :r * half_t]), decay], axis=1)
                    halves.append(_dot((cbm * decay).astype(BF16), x_slab))
                slabs.append(jnp.where(lane_lo, halves[0], halves[1]))
            y_rows.append(jnp.concatenate(slabs, axis=1))
        y_parts.append(jnp.concatenate(y_rows, axis=0) + y_off)
    y = jnp.concatenate(y_parts, axis=1) + dskip_ref[...] * xs
    y = y * zs_ref[...]
    norm_w = nw_ref[...]
    gn = SSM_WIDTH // SSM_GROUPS
    for g in range(SSM_GROUPS):
        cols = slice(g * gn, (g + 1) * gn)
        o_ref[:, cols] = (_rms(y[:, cols]) * norm_w[:, cols]).astype(o_ref.dtype)


def _ssd(xs, bc, zs, dt, a_log, d_skip, norm_w):
    T = SSM_CHUNK
    row = lambda c: (c, 0)
    fixed = lambda c: (0, 0)
    full = lambda a: pl.BlockSpec(a.shape, fixed)
    return pl.pallas_call(
        _ssd_kernel,
        grid=(SEQ // T,),
        in_specs=[
            pl.BlockSpec((T, SSM_WIDTH), row),
            pl.BlockSpec((T, 2 * SSM_BC), row),
            pl.BlockSpec((T, SSM_WIDTH), row),
            pl.BlockSpec((T, LANES), row),
            full(a_log), full(d_skip), full(norm_w),
        ],
        out_specs=pl.BlockSpec((T, SSM_WIDTH), row),
        out_shape=jax.ShapeDtypeStruct((SEQ, SSM_WIDTH), BF16),
        scratch_shapes=[pltpu.VMEM((SSM_STATE, SSM_WIDTH), F32)],
        compiler_params=pltpu.CompilerParams(
            dimension_semantics=("arbitrary",), vmem_limit_bytes=VMEM_LIMIT_BYTES),
        name="ssd",
    )(xs, bc, zs, dt, a_log, d_skip, norm_w)


def _ffn_kernel(x_ref, a_ref, s_ref, g1_ref, pmw_ref, pfw_ref, sc2_ref, sh2_ref, g2_ref, pow_ref,
                wout_ref, wup_ref, cw_ref, cb_ref, wdown_ref, o_ref,
                h_ref, acc_ref, stage_ref, carry_ref):
    tm = TOKEN_TILE
    i = pl.program_id(0)
    n_slabs = D_MODEL // LANES

    @pl.when(i == 0)
    def _():
        carry_ref[...] = jnp.zeros_like(carry_ref)

    mixed = _dot(a_ref[...], wout_ref[:ATTN_WIDTH, :]) + _dot(s_ref[...], wout_ref[ATTN_WIDTH:, :])
    x1 = x_ref[...] + g1_ref[...] * (_rms(mixed) * pmw_ref[...])
    o_ref[...] = x1
    h = _rms(x1) * pfw_ref[...]
    h = h * (1.0 + sc2_ref[...]) + sh2_ref[...]
    for l in range(n_slabs):
        for s in range(SUBLANES):
            stage_ref[l, s * SEG_PITCH:s * SEG_PITCH + SEG, :] = (
                h[s * SEG:(s + 1) * SEG, l * LANES:(l + 1) * LANES])
    for l in range(n_slabs):
        rows = [stage_ref[l, pl.ds(v, SUBLANES, stride=SEG_PITCH), :] for v in range(SEG)]
        h_ref[:, l * LANES:(l + 1) * LANES] = jnp.concatenate(rows, axis=0).astype(BF16)
    acc_ref[...] = jnp.zeros_like(acc_ref)

    last_sublane = lax.broadcasted_iota(jnp.int32, (SUBLANES, FF_BLOCK), 0) == SUBLANES - 1

    def conv(u, cols):
        tail = u[tm - 2 * SUBLANES:, :]
        prev_tail = carry_ref[:, cols]
        carry_ref[:, cols] = tail
        before = [pltpu.roll(jnp.where(last_sublane, prev_tail[r * SUBLANES:(r + 1) * SUBLANES, :],
                                       tail[r * SUBLANES:(r + 1) * SUBLANES, :]), 1, axis=0)
                  for r in range(FFN_CONV - 1)]
        u_m1 = jnp.concatenate([before[1], u[:tm - SUBLANES, :]], axis=0)
        u_m2 = jnp.concatenate([before[0], before[1], u[:tm - 2 * SUBLANES, :]], axis=0)
        return (cb_ref[:, cols] + cw_ref[2:3, cols] * u + cw_ref[1:2, cols] * u_m1
                + cw_ref[0:1, cols] * u_m2)

    n_blocks = D_FF // FF_BLOCK
    gate_cols = lambda j: slice(j * FF_BLOCK, (j + 1) * FF_BLOCK)
    val_cols = lambda j: slice(D_FF + j * FF_BLOCK, D_FF + (j + 1) * FF_BLOCK)

    def up(j):
        hb = h_ref[...]
        return _dot(hb, wup_ref[:, gate_cols(j)]), _dot(hb, wup_ref[:, val_cols(j)])

    def down(j0, fs):
        f = fs[0] if len(fs) == 1 else jnp.concatenate(fs, axis=1)
        acc_ref[...] += _dot(f, wdown_ref[j0 * FF_BLOCK:(j0 + len(fs)) * FF_BLOCK, :])

    def gated(g, v):
        c = math.sqrt(2.0 / math.pi)
        t = jnp.tanh(g * (c + (c * 0.044715) * (g * g)))
        hv = (0.5 * g) * v
        return hv + hv * t

    nxt = up(0)
    pending = []
    for j in range(n_blocks):
        cur = nxt
        if j + 1 < n_blocks:
            nxt = up(j + 1)
        if len(pending) == DOWN_GROUP:
            down(j - DOWN_GROUP, pending)
            pending = []
        u_gate = conv(cur[0], gate_cols(j))
        u_val = conv(cur[1], val_cols(j))
        pending.append(gated(u_gate, u_val).astype(BF16))
    down(n_blocks - len(pending), pending)

    y = g2_ref[...] * (_rms(acc_ref[...]) * pow_ref[...])
    for l in range(n_slabs):
        stage_ref[l, 0:tm, :] = y[:, l * LANES:(l + 1) * LANES]
    for l in range(n_slabs):
        rows = [stage_ref[l, pl.ds(k // SUBLANES + SEG * (k % SUBLANES), SUBLANES, stride=SUBLANES), :]
                for k in range(SEG)]
        o_ref[:, l * LANES:(l + 1) * LANES] += jnp.concatenate(rows, axis=0)


def _out_ffn(x, attn, ssm, g1, pmw, pfw, sc2, sh2, g2, pow_, w_out, w_up, cw, cb, w_down):
    tm = TOKEN_TILE
    row = lambda i: (i, 0)
    fixed = lambda i: (0, 0)
    vec = pl.BlockSpec((1, D_MODEL), fixed)
    resident = lambda a: pl.BlockSpec(a.shape, fixed, pipeline_mode=pl.Buffered(1))
    return pl.pallas_call(
        _ffn_kernel,
        grid=(SEQ // tm,),
        in_specs=[
            pl.BlockSpec((tm, D_MODEL), row),
            pl.BlockSpec((tm, ATTN_WIDTH), row),
            pl.BlockSpec((tm, SSM_WIDTH), row),
            vec, vec, vec, vec, vec, vec, vec,
            resident(w_out), resident(w_up), resident(cw), resident(cb), resident(w_down),
        ],
        out_specs=pl.BlockSpec((tm, D_MODEL), row),
        out_shape=jax.ShapeDtypeStruct((SEQ, D_MODEL), F32),
        scratch_shapes=[
            pltpu.VMEM((tm, D_MODEL), BF16),
            pltpu.VMEM((tm, D_MODEL), F32),
            pltpu.VMEM((D_MODEL // LANES, SUBLANES * SEG_PITCH, LANES), F32),
            pltpu.VMEM(((FFN_CONV - 1) * SUBLANES, 2 * D_FF), F32),
        ],
        compiler_params=pltpu.CompilerParams(
            dimension_semantics=("arbitrary",), vmem_limit_bytes=VMEM_LIMIT_BYTES),
        name="ffn",
    )(x, attn, ssm, g1, pmw, pfw, sc2, sh2, g2, pow_, w_out, w_up, cw, cb, w_down)


def _pad_lanes(a):
    return jnp.pad(a, ((0, 0), (0, LANES - a.shape[-1])))


def kernel(x, c, rel_bias, w_ada, b_ada, pre_mix_w, w_in, attn_sinks, ssm_conv_w, ssm_conv_b,
           ssm_dt_bias, ssm_a_log, ssm_d, ssm_norm_w, w_out, post_mix_w, pre_ffn_w, w_up,
           ffn_conv_w, ffn_conv_b, w_down, post_ffn_w):
    bsz, s, d = x.shape
    assert (bsz, s, d) == (1, SEQ, D_MODEL) and w_ada.shape[0] == 1
    x2 = x.reshape(s, d)

    mod = _modulation(c.reshape(d, 1), w_ada[0], b_ada[0].reshape(1, -1))
    shift1, scale1, gate1, shift2, scale2, gate2 = jnp.split(mod, N_MOD, axis=-1)
    bias = _bias_table(rel_bias)

    q, kv, xs, bc, zs, dt, w_out_b, w_up_b, w_down_b = _in_proj(
        x2, pre_mix_w, scale1, shift1, w_in[0], ssm_conv_w[0], ssm_conv_b, _pad_lanes(ssm_dt_bias),
        w_out[0], w_up[0], w_down[0].reshape(D_MODEL, D_FF))

    attn = _attention(attn_sinks, q, kv, bias)
    ssm = _ssd(xs, bc, zs, dt, _pad_lanes(ssm_a_log), jnp.repeat(ssm_d, SSM_HEAD_DIM, axis=1),
               ssm_norm_w)

    out = _out_ffn(x2, attn, ssm, gate1, post_mix_w, pre_ffn_w, scale2, shift2, gate2, post_ffn_w,
                   w_out_b, w_up_b, ffn_conv_w[0], ffn_conv_b, w_down_b.reshape(D_FF, D_MODEL))
    return out.reshape(bsz, s, d)
```

```python
import math

import numpy as np
import jax
import jax.numpy as jnp
from jax import lax
from jax.experimental import pallas as pl
from jax.experimental.pallas import tpu as pltpu

D_MODEL = 1024
SEQ = 16384
N_MOD = 6

ATTN_Q_HEADS = 8
ATTN_KV_HEADS = 2
HEAD_DIM = 64
ATTN_WIDTH = ATTN_Q_HEADS * HEAD_DIM
KV_WIDTH = ATTN_KV_HEADS * HEAD_DIM
WINDOW = 128
ATTN_BLOCK = 128
N_REL_BUCKETS = 32
REL_MAX_DISTANCE = 128

SSM_HEADS = 8
SSM_HEAD_DIM = 64
SSM_WIDTH = SSM_HEADS * SSM_HEAD_DIM
SSM_STATE = 128
SSM_GROUPS = 2
SSM_BC = SSM_GROUPS * SSM_STATE
SSM_CONV = 4
SSM_CHUNK = 256
XBC_WIDTH = SSM_WIDTH + 2 * SSM_BC

D_FF = 2816
FFN_CONV = 3
NORM_EPS = 1e-6
MASK_VALUE = -1e30

LANES = 128
SUBLANES = 8
VMEM_LIMIT_BYTES = 56 * 1024 * 1024

TOKEN_TILE = 512
FF_BLOCK = 256
DOWN_GROUP = 2
SSM_CHUNKS_PER_STEP = 4
HEADS_PER_SLAB = LANES // HEAD_DIM
SEG = TOKEN_TILE // SUBLANES
SEG_PITCH = SEG + SUBLANES

BF16 = jnp.bfloat16
F32 = jnp.float32


def _dot(a, b):
    return jnp.dot(a, b, preferred_element_type=F32)


def _dot_nt(a, b):
    return lax.dot_general(a, b, (((1,), (1,)), ((), ())), preferred_element_type=F32)


def _dot_tn(a, b):
    return lax.dot_general(a, b, (((0,), (0,)), ((), ())), preferred_element_type=F32)


def _split3(x):
    x1 = x.astype(BF16)
    r1 = x - x1.astype(F32)
    x2 = r1.astype(BF16)
    r2 = r1 - x2.astype(F32)
    return x1, x2, r2.astype(BF16)


def _dot_select(sel, x):
    x1, x2, x3 = _split3(x)
    return _dot(sel, x1) + _dot(sel, x2) + _dot(sel, x3)


def _select_dot(x, sel):
    x1 = x.astype(BF16)
    x2 = (x - x1.astype(F32)).astype(BF16)
    return _dot(x1, sel) + _dot(x2, sel)


def _rms(x):
    return x * lax.rsqrt(jnp.mean(x * x, axis=-1, keepdims=True) + NORM_EPS)


def _mod_kernel(c_ref, w_ref, b_ref, o_ref):
    cond = jax.nn.silu(c_ref[...])
    o_ref[...] = jnp.sum(cond * w_ref[...], axis=0, keepdims=True) + b_ref[...]


def _modulation(c_col, w_ada, b_ada):
    return pl.pallas_call(
        _mod_kernel,
        grid=(N_MOD,),
        in_specs=[
            pl.BlockSpec((D_MODEL, 1), lambda j: (0, 0)),
            pl.BlockSpec((None, D_MODEL, D_MODEL), lambda j: (0, 0, j)),
            pl.BlockSpec((1, D_MODEL), lambda j: (0, j)),
        ],
        out_specs=pl.BlockSpec((1, D_MODEL), lambda j: (0, j)),
        out_shape=jax.ShapeDtypeStruct((1, N_MOD * D_MODEL), F32),
        name="mod",
    )(c_col, w_ada, b_ada)


def _t5_bucket_table():
    L = ATTN_BLOCK
    dist = np.maximum((np.arange(L)[:, None] + L) - np.arange(2 * L)[None, :], 0)
    max_exact = N_REL_BUCKETS // 2
    nf = np.maximum(dist, 1).astype(np.float64)
    large = max_exact + (np.log(nf / max_exact) / math.log(REL_MAX_DISTANCE / max_exact)
                         * (N_REL_BUCKETS - max_exact)).astype(np.int32)
    large = np.minimum(large, N_REL_BUCKETS - 1)
    return np.where(dist < max_exact, dist, large).astype(np.int32)


def _bias_kernel(rb_ref, bucket_ref, o_ref):
    L = ATTN_BLOCK
    heads_per_kv = ATTN_Q_HEADS // ATTN_KV_HEADS
    bucket = bucket_ref[...]
    qi = lax.broadcasted_iota(jnp.int32, (L, 2 * L), 0) + L
    kj = lax.broadcasted_iota(jnp.int32, (L, 2 * L), 1)
    dist = qi - kj
    for h in range(ATTN_Q_HEADS):
        acc = jnp.zeros((L, 2 * L), F32)
        for b in range(N_REL_BUCKETS):
            acc = jnp.where(bucket == b, rb_ref[b, h], acc)
        band = jnp.where(dist >= 0, jnp.where(dist < WINDOW, acc, MASK_VALUE), MASK_VALUE)
        kvh, s, half = h // heads_per_kv, (h % heads_per_kv) // HEADS_PER_SLAB, h % HEADS_PER_SLAB
        rows = slice(s * L, (s + 1) * L)
        cols = slice(half * 2 * L, (half + 1) * 2 * L)
        o_ref[1, kvh, rows, cols] = band
        o_ref[0, kvh, rows, cols] = jnp.where(kj >= L, band, MASK_VALUE)


def _bias_table(rel_bias):
    L = ATTN_BLOCK
    shape = (2, ATTN_KV_HEADS, ATTN_Q_HEADS // ATTN_KV_HEADS // HEADS_PER_SLAB * L,
             HEADS_PER_SLAB * 2 * L)
    return pl.pallas_call(
        _bias_kernel,
        in_specs=[
            pl.BlockSpec(memory_space=pltpu.SMEM),
            pl.BlockSpec((L, 2 * L), lambda: (0, 0)),
        ],
        out_specs=pl.BlockSpec(shape, lambda: (0, 0, 0, 0)),
        out_shape=jax.ShapeDtypeStruct(shape, F32),
        name="bias",
    )(rel_bias, jnp.asarray(_t5_bucket_table()))


def _dup_heads(t):
    lo = lax.broadcasted_iota(jnp.int32, t.shape, 1) < HEAD_DIM
    swapped = pltpu.roll(t, HEAD_DIM, axis=1)
    return jnp.where(lo, t, swapped), jnp.where(lo, swapped, t)


_IN_Q, _IN_K, _IN_V, _IN_XS, _IN_Z, _IN_B, _IN_C, _IN_DT, IN_PROJ_WIDTH = np.cumsum(
    (0, ATTN_WIDTH, KV_WIDTH, KV_WIDTH, SSM_WIDTH, SSM_WIDTH, SSM_BC, SSM_BC, SSM_HEADS)).tolist()
_W_Q, _W_KV, _W_XBC, _W_Z, _W_DT, _W_END = np.cumsum(
    (0, ATTN_WIDTH, 2 * KV_WIDTH, XBC_WIDTH, SSM_WIDTH, LANES)).tolist()
CAST_ROWS = D_MODEL // (SEQ // TOKEN_TILE)
DOWN_CAST_STEPS = 16
DOWN_CAST_ROWS = D_FF // DOWN_CAST_STEPS


def _inproj_kernel(x_ref, nw_ref, sc_ref, sh_ref, win_ref, cw_ref, cb_ref, dtb_ref,
                   wout_ref, wup_ref, wdown_ref,
                   q_ref, kv_ref, xs_ref, bc_ref, zs_ref, dt_ref, wout_b_ref, wup_b_ref, wdown_b_ref,
                   w_ref, carry_ref):
    i = pl.program_id(0)

    @pl.when(i == 0)
    def _():
        w_ref[:, _W_Q:_W_KV] = win_ref[:, _IN_Q:_IN_K].astype(BF16)
        w_ref[:, _W_KV:_W_XBC] = win_ref[:, _IN_K:_IN_XS].astype(BF16)
        w_ref[:, _W_XBC:_W_XBC + SSM_WIDTH] = win_ref[:, _IN_XS:_IN_Z].astype(BF16)
        w_ref[:, _W_XBC + SSM_WIDTH:_W_Z] = win_ref[:, _IN_B:_IN_DT].astype(BF16)
        w_ref[:, _W_Z:_W_DT] = win_ref[:, _IN_Z:_IN_B].astype(BF16)
        w_ref[:, _W_DT:_W_END] = jnp.zeros((D_MODEL, LANES), BF16)
        w_ref[:, _W_DT:_W_DT + SSM_HEADS] = win_ref[:, _IN_DT:IN_PROJ_WIDTH].astype(BF16)
        carry_ref[...] = jnp.zeros_like(carry_ref)

    h = _rms(x_ref[...]) * nw_ref[...]
    h = (h * (1.0 + sc_ref[...]) + sh_ref[...]).astype(BF16)

    def conv_silu(u, cols):
        padded = jnp.concatenate([carry_ref[:, cols], u], axis=0)
        carry_ref[:, cols] = u[TOKEN_TILE - SUBLANES:, :]
        acc = cb_ref[:, cols] + cw_ref[SSM_CONV - 1:SSM_CONV, cols] * u
        for k in range(SSM_CONV - 1):
            shifted = pltpu.roll(padded, SSM_CONV - 1 - k, axis=0)[SUBLANES:, :]
            acc = acc + cw_ref[k:k + 1, cols] * shifted
        return jax.nn.silu(acc)

    u_xs = _dot(h, w_ref[:, _W_XBC:_W_XBC + SSM_WIDTH])
    u_bc = _dot(h, w_ref[:, _W_XBC + SSM_WIDTH:_W_Z])
    u_z = _dot(h, w_ref[:, _W_Z:_W_DT])
    u_q = _dot(h, w_ref[:, _W_Q:_W_KV])
    kv = _dot(h, w_ref[:, _W_KV:_W_XBC])
    u_dt = _dot(h, w_ref[:, _W_DT:_W_END])
    wout_b_ref[...] = wout_ref[...].astype(BF16)
    wup_b_ref[...] = wup_ref[...].astype(BF16)
    wdown_b_ref[...] = wdown_ref[...].astype(BF16)
    xs_ref[...] = conv_silu(u_xs, slice(0, SSM_WIDTH))
    bc_ref[...] = conv_silu(u_bc, slice(SSM_WIDTH, XBC_WIDTH)).astype(BF16)
    zs_ref[...] = jax.nn.silu(u_z)
    q_ref[...] = (u_q * (HEAD_DIM ** -0.5)).astype(BF16)
    k0, k1 = _dup_heads(kv[:, :KV_WIDTH])
    v0, v1 = _dup_heads(kv[:, KV_WIDTH:])
    kv_ref[:, 0 * LANES:1 * LANES] = k0.astype(BF16)
    kv_ref[:, 1 * LANES:2 * LANES] = k1.astype(BF16)
    kv_ref[:, 2 * LANES:3 * LANES] = v0.astype(BF16)
    kv_ref[:, 3 * LANES:4 * LANES] = v1.astype(BF16)
    dt_ref[...] = jax.nn.softplus(u_dt + dtb_ref[...])


def _in_proj(x, nw, scale, shift, w_in, conv_w, conv_b, dt_bias, w_out, w_up, w_down):
    tm = TOKEN_TILE
    row = lambda i: (i, 0)
    row3 = lambda i: (0, i, 0)
    down_row = lambda i: (jnp.minimum(i, DOWN_CAST_STEPS - 1), 0)
    down_row3 = lambda i: (0, jnp.minimum(i, DOWN_CAST_STEPS - 1), 0)
    fixed = lambda i: (0, 0)
    vec = pl.BlockSpec((1, D_MODEL), fixed)
    full = lambda a: pl.BlockSpec(a.shape, fixed)
    return pl.pallas_call(
        _inproj_kernel,
        grid=(SEQ // tm,),
        in_specs=[pl.BlockSpec((None, tm, D_MODEL), row3), vec, vec, vec,
                  pl.BlockSpec((None,) + w_in.shape[1:], lambda i: (0, 0, 0),
                               pipeline_mode=pl.Buffered(1)),
                  full(conv_w), full(conv_b), full(dt_bias),
                  pl.BlockSpec((None, CAST_ROWS, w_out.shape[2]), row3),
                  pl.BlockSpec((None, CAST_ROWS, w_up.shape[2]), row3),
                  pl.BlockSpec((None, DOWN_CAST_ROWS, w_down.shape[2]), down_row3)],
        out_specs=[
            pl.BlockSpec((tm, ATTN_WIDTH), row),
            pl.BlockSpec((tm, 4 * LANES), row),
            pl.BlockSpec((tm, SSM_WIDTH), row),
            pl.BlockSpec((tm, 2 * SSM_BC), row),
            pl.BlockSpec((tm, SSM_WIDTH), row),
            pl.BlockSpec((tm, LANES), row),
            pl.BlockSpec((CAST_ROWS, w_out.shape[2]), row),
            pl.BlockSpec((CAST_ROWS, w_up.shape[2]), row),
            pl.BlockSpec((DOWN_CAST_ROWS, w_down.shape[2]), down_row),
        ],
        out_shape=[
            jax.ShapeDtypeStruct((SEQ, ATTN_WIDTH), BF16),
            jax.ShapeDtypeStruct((SEQ, 4 * LANES), BF16),
            jax.ShapeDtypeStruct((SEQ, SSM_WIDTH), F32),
            jax.ShapeDtypeStruct((SEQ, 2 * SSM_BC), BF16),
            jax.ShapeDtypeStruct((SEQ, SSM_WIDTH), F32),
            jax.ShapeDtypeStruct((SEQ, LANES), F32),
            jax.ShapeDtypeStruct(w_out.shape[1:], BF16),
            jax.ShapeDtypeStruct(w_up.shape[1:], BF16),
            jax.ShapeDtypeStruct(w_down.shape[1:], BF16),
        ],
        scratch_shapes=[
            pltpu.VMEM((D_MODEL, _W_END), BF16),
            pltpu.VMEM((SUBLANES, XBC_WIDTH), F32),
        ],
        compiler_params=pltpu.CompilerParams(
            dimension_semantics=("arbitrary",), vmem_limit_bytes=VMEM_LIMIT_BYTES),
        name="in_proj",
    )(x, nw, scale, shift, w_in, conv_w, conv_b, dt_bias, w_out, w_up, w_down)


def _attn_kernel(sink_ref, q_ref, kv_ref, kvprev_ref, bias_ref, o_ref, band_ref):
    L = ATTN_BLOCK
    i = pl.program_id(0)
    band_ref[0:L, :] = kvprev_ref[...]
    band_ref[L:, :] = kv_ref[...]
    lane_lo = lax.broadcasted_iota(jnp.int32, (2 * L, LANES), 1) < HEAD_DIM
    out_lo = lax.broadcasted_iota(jnp.int32, (L, LANES), 1) < HEAD_DIM
    zero = jnp.zeros((2 * L, LANES), BF16)
    slabs_per_kv = ATTN_Q_HEADS // ATTN_KV_HEADS // HEADS_PER_SLAB

    def block_diag(t):
        return jnp.concatenate([jnp.where(lane_lo, t, zero), jnp.where(lane_lo, zero, t)], axis=0)

    def scores(b, kvh):
        r0 = b * L
        k2 = block_diag(band_ref[r0:r0 + 2 * L, kvh * LANES:(kvh + 1) * LANES])
        q2 = jnp.concatenate(
            [q_ref[r0:r0 + L, (kvh * slabs_per_kv + s) * LANES:(kvh * slabs_per_kv + s + 1) * LANES]
             for s in range(slabs_per_kv)], axis=0)
        table = jnp.where(i == 0, 0, 1) if b == 0 else 1
        return _dot_nt(q2, k2) + bias_ref[table, kvh]

    def softmax(kvh, sc):
        rows, scales = [], []
        for s in range(slabs_per_kv):
            ps, inv = [], []
            for half in range(HEADS_PER_SLAB):
                sink = sink_ref[0, (kvh * slabs_per_kv + s) * HEADS_PER_SLAB + half]
                sh = sc[s * L:(s + 1) * L, half * 2 * L:(half + 1) * 2 * L]
                m = jnp.maximum(jnp.max(sh, axis=-1, keepdims=True), sink)
                p = jnp.exp(sh - m)
                denom = jnp.sum(p, axis=-1, keepdims=True) + jnp.exp(sink - m)
                ps.append(p.astype(BF16))
                inv.append(1.0 / denom)
            rows.append(jnp.concatenate(ps, axis=1))
            scales.append(jnp.where(out_lo, inv[0], inv[1]))
        return jnp.concatenate(rows, axis=0), jnp.concatenate(scales, axis=0)

    def weighted_values(b, kvh, p, scale):
        r0 = b * L
        v2 = block_diag(band_ref[r0:r0 + 2 * L, (ATTN_KV_HEADS + kvh) * LANES:
                                 (ATTN_KV_HEADS + kvh + 1) * LANES])
        o = (_dot(p, v2) * scale).astype(o_ref.dtype)
        for s in range(slabs_per_kv):
            slab = kvh * slabs_per_kv + s
            o_ref[r0:r0 + L, slab * LANES:(slab + 1) * LANES] = o[s * L:(s + 1) * L, :]

    units = [(b, kvh) for b in range(TOKEN_TILE // L) for kvh in range(ATTN_KV_HEADS)]
    nxt = scores(*units[0])
    for n, (b, kvh) in enumerate(units):
        cur = nxt
        if n + 1 < len(units):
            nxt = scores(*units[n + 1])
        p, scale = softmax(kvh, cur)
        weighted_values(b, kvh, p, scale)


def _attention(sinks, q, kv, bias):
    tm, L = TOKEN_TILE, ATTN_BLOCK
    blocks_per_tile = tm // L
    row = lambda i: (i, 0)
    return pl.pallas_call(
        _attn_kernel,
        grid=(SEQ // tm,),
        in_specs=[
            pl.BlockSpec(memory_space=pltpu.SMEM),
            pl.BlockSpec((tm, ATTN_WIDTH), row),
            pl.BlockSpec((tm, 4 * LANES), row),
            pl.BlockSpec((L, 4 * LANES), lambda i: (jnp.maximum(i * blocks_per_tile - 1, 0), 0)),
            pl.BlockSpec(bias.shape, lambda i: (0, 0, 0, 0)),
        ],
        out_specs=pl.BlockSpec((tm, ATTN_WIDTH), row),
        out_shape=jax.ShapeDtypeStruct((SEQ, ATTN_WIDTH), BF16),
        scratch_shapes=[pltpu.VMEM((L + tm, 4 * LANES), BF16)],
        compiler_params=pltpu.CompilerParams(
            dimension_semantics=("parallel",), vmem_limit_bytes=VMEM_LIMIT_BYTES),
        name="attn",
    )(sinks, q, kv, kv, bias)


def _ssd_kernel(xs_ref, bc_ref, zs_ref, dt_ref, alog_ref, dskip_ref, nw_ref, o_ref, state_ref):
    T = SSM_CHUNK
    half_t = T // 2
    heads_per_group = SSM_HEADS // SSM_GROUPS
    slabs_per_group = heads_per_group // HEADS_PER_SLAB
    gw = heads_per_group * SSM_HEAD_DIM
    gn = SSM_WIDTH // SSM_GROUPS

    @pl.when(pl.program_id(0) == 0)
    def _():
        state_ref[...] = jnp.zeros_like(state_ref)

    neg_a = -jnp.exp(alog_ref[...])
    ti = lax.broadcasted_iota(jnp.int32, (T, T), 0)
    tj = lax.broadcasted_iota(jnp.int32, (T, T), 1)
    cumsum_mat = jnp.where(ti >= tj, 1.0, 0.0).astype(BF16)
    causal_half = (lax.broadcasted_iota(jnp.int32, (half_t, half_t), 0)
                   >= lax.broadcasted_iota(jnp.int32, (half_t, half_t), 1))
    er = lax.broadcasted_iota(jnp.int32, (LANES, SSM_WIDTH), 0)
    ec = lax.broadcasted_iota(jnp.int32, (LANES, SSM_WIDTH), 1)
    expand = jnp.where(jnp.logical_and(ec >= er * SSM_HEAD_DIM, ec < (er + 1) * SSM_HEAD_DIM),
                       1.0, 0.0).astype(BF16)
    lane_lo = lax.broadcasted_iota(jnp.int32, (half_t, LANES), 1) < SSM_HEAD_DIM

    chunks = range(SSM_CHUNKS_PER_STEP)
    row0 = [k * T for k in chunks]
    xs = [xs_ref[r0:r0 + T, :] for r0 in row0]
    dt = [dt_ref[r0:r0 + T, :] for r0 in row0]
    acum = [_dot_select(cumsum_mat, dt[k] * neg_a) for k in chunks]
    key_t = [(acum[k] - jnp.log(dt[k])).T for k in chunks]
    from_start = [_select_dot(jnp.exp(acum[k]), expand) for k in chunks]
    x_weight = [_select_dot(dt[k] * jnp.exp(acum[k][T - 1:T, :] - acum[k]), expand)
                for k in chunks]
    xs_b = [xs[k].astype(BF16) for k in chunks]
    xce_b = [(xs[k] * x_weight[k]).astype(BF16) for k in chunks]

    def b_of(k, g):
        return bc_ref[row0[k]:row0[k] + T, g * SSM_STATE:(g + 1) * SSM_STATE]

    def c_of(k, g):
        return bc_ref[row0[k]:row0[k] + T, SSM_BC + g * SSM_STATE:SSM_BC + (g + 1) * SSM_STATE]

    new_states = [[_dot_tn(b_of(k, g), xce_b[k][:, g * gw:(g + 1) * gw]) for g in range(SSM_GROUPS)]
                  for k in chunks]
    y_diag = [[[None, None] for _ in range(SSM_GROUPS)] for _ in chunks]
    for g in range(SSM_GROUPS):
        for r in range(2):
            rows = slice(r * half_t, (r + 1) * half_t)
            n_keys = (r + 1) * half_t
            cbm = [_dot_nt(c_of(k, g)[rows, :], b_of(k, g)[:n_keys, :]) for k in chunks]
            slabs = [[] for _ in chunks]
            for s in range(slabs_per_group):
                slab = g * slabs_per_group + s
                halves = [[] for _ in chunks]
                for half in range(HEADS_PER_SLAB):
                    h = slab * HEADS_PER_SLAB + half
                    for k in chunks:
                        diff = acum[k][rows, h:h + 1] - key_t[k][h:h + 1, :n_keys]
                        decay = jnp.exp(jnp.where(causal_half, diff[:, r * half_t:], MASK_VALUE))
                        if r:
                            decay = jnp.concatenate([jnp.exp(diff[:, :r * half_t]), decay], axis=1)
                        halves[k].append(_dot((cbm[k] * decay).astype(BF16),
                                              xs_b[k][:n_keys, slab * LANES:(slab + 1) * LANES]))
                for k in chunks:
                    slabs[k].append(jnp.where(lane_lo, halves[k][0], halves[k][1]))
            for k in chunks:
                y_diag[k][g][r] = jnp.concatenate(slabs[k], axis=1)

    state = [state_ref[:, g * gw:(g + 1) * gw] for g in range(SSM_GROUPS)]
    norm_w = nw_ref[...]
    for k in chunks:
        r0 = row0[k]
        chunk_decay = from_start[k][T - 1:T, :]
        y_parts = []
        for g in range(SSM_GROUPS):
            gcols = slice(g * gw, (g + 1) * gw)
            y_off = _dot(c_of(k, g), state[g].astype(BF16)) * from_start[k][:, gcols]
            y_parts.append(jnp.concatenate(y_diag[k][g], axis=0) + y_off)
            state[g] = state[g] * chunk_decay[:, gcols] + new_states[k][g]
        y = jnp.concatenate(y_parts, axis=1) + dskip_ref[...] * xs[k]
        y = y * zs_ref[r0:r0 + T, :]
        for g in range(SSM_GROUPS):
            cols = slice(g * gn, (g + 1) * gn)
            o_ref[r0:r0 + T, cols] = (_rms(y[:, cols]) * norm_w[:, cols]).astype(o_ref.dtype)
    for g in range(SSM_GROUPS):
        state_ref[:, g * gw:(g + 1) * gw] = state[g]


def _ssd(xs, bc, zs, dt, a_log, d_skip, norm_w):
    rows = SSM_CHUNKS_PER_STEP * SSM_CHUNK
    row = lambda c: (c, 0)
    fixed = lambda c: (0, 0)
    full = lambda a: pl.BlockSpec(a.shape, fixed)
    return pl.pallas_call(
        _ssd_kernel,
        grid=(SEQ // rows,),
        in_specs=[
            pl.BlockSpec((rows, SSM_WIDTH), row),
            pl.BlockSpec((rows, 2 * SSM_BC), row),
            pl.BlockSpec((rows, SSM_WIDTH), row),
            pl.BlockSpec((rows, LANES), row),
            full(a_log), full(d_skip), full(norm_w),
        ],
        out_specs=pl.BlockSpec((rows, SSM_WIDTH), row),
        out_shape=jax.ShapeDtypeStruct((SEQ, SSM_WIDTH), BF16),
        scratch_shapes=[pltpu.VMEM((SSM_STATE, SSM_WIDTH), F32)],
        compiler_params=pltpu.CompilerParams(
            dimension_semantics=("arbitrary",), vmem_limit_bytes=VMEM_LIMIT_BYTES),
        name="ssd",
    )(xs, bc, zs, dt, a_log, d_skip, norm_w)


def _ffn_kernel(x_ref, a_ref, s_ref, g1_ref, pmw_ref, pfw_ref, sc2_ref, sh2_ref, g2_ref, pow_ref,
                wout_ref, wup_ref, cw_ref, cb_ref, wdown_ref, o_ref,
                h_ref, acc_ref, stage_ref, carry_ref):
    tm = TOKEN_TILE
    i = pl.program_id(0)
    n_slabs = D_MODEL // LANES

    @pl.when(i == 0)
    def _():
        carry_ref[...] = jnp.zeros_like(carry_ref)

    mixed = _dot(a_ref[...], wout_ref[:ATTN_WIDTH, :]) + _dot(s_ref[...], wout_ref[ATTN_WIDTH:, :])
    x1 = x_ref[...] + g1_ref[...] * (_rms(mixed) * pmw_ref[...])
    o_ref[...] = x1
    h = _rms(x1) * pfw_ref[...]
    h = h * (1.0 + sc2_ref[...]) + sh2_ref[...]
    for l in range(n_slabs):
        for s in range(SUBLANES):
            stage_ref[l, s * SEG_PITCH:s * SEG_PITCH + SEG, :] = (
                h[s * SEG:(s + 1) * SEG, l * LANES:(l + 1) * LANES])
    for l in range(n_slabs):
        rows = [stage_ref[l, pl.ds(v, SUBLANES, stride=SEG_PITCH), :] for v in range(SEG)]
        h_ref[:, l * LANES:(l + 1) * LANES] = jnp.concatenate(rows, axis=0).astype(BF16)
    acc_ref[...] = jnp.zeros_like(acc_ref)

    last_sublane = lax.broadcasted_iota(jnp.int32, (SUBLANES, FF_BLOCK), 0) == SUBLANES - 1

    def conv(u, cols):
        tail = u[tm - 2 * SUBLANES:, :]
        prev_tail = carry_ref[:, cols]
        carry_ref[:, cols] = tail
        before = [pltpu.roll(jnp.where(last_sublane, prev_tail[r * SUBLANES:(r + 1) * SUBLANES, :],
                                       tail[r * SUBLANES:(r + 1) * SUBLANES, :]), 1, axis=0)
                  for r in range(FFN_CONV - 1)]
        u_m1 = jnp.concatenate([before[1], u[:tm - SUBLANES, :]], axis=0)
        u_m2 = jnp.concatenate([before[0], before[1], u[:tm - 2 * SUBLANES, :]], axis=0)
        return (cb_ref[:, cols] + cw_ref[2:3, cols] * u + cw_ref[1:2, cols] * u_m1
                + cw_ref[0:1, cols] * u_m2)

    n_blocks = D_FF // FF_BLOCK
    gate_cols = lambda j: slice(j * FF_BLOCK, (j + 1) * FF_BLOCK)
    val_cols = lambda j: slice(D_FF + j * FF_BLOCK, D_FF + (j + 1) * FF_BLOCK)

    def up(j):
        hb = h_ref[...]
        return _dot(hb, wup_ref[:, gate_cols(j)]), _dot(hb, wup_ref[:, val_cols(j)])

    def down(j0, fs):
        f = fs[0] if len(fs) == 1 else jnp.concatenate(fs, axis=1)
        acc_ref[...] += _dot(f, wdown_ref[j0 * FF_BLOCK:(j0 + len(fs)) * FF_BLOCK, :])

    def gated(g, v):
        c = math.sqrt(2.0 / math.pi)
        t = jnp.tanh(g * (c + (c * 0.044715) * (g * g)))
        hv = (0.5 * g) * v
        return hv + hv * t

    nxt = up(0)
    pending = []
    for j in range(n_blocks):
        cur = nxt
        if j + 1 < n_blocks:
            nxt = up(j + 1)
        if len(pending) == DOWN_GROUP:
            down(j - DOWN_GROUP, pending)
            pending = []
        u_gate = conv(cur[0], gate_cols(j))
        u_val = conv(cur[1], val_cols(j))
        pending.append(gated(u_gate, u_val).astype(BF16))
    down(n_blocks - len(pending), pending)

    y = g2_ref[...] * (_rms(acc_ref[...]) * pow_ref[...])
    for l in range(n_slabs):
        stage_ref[l, 0:tm, :] = y[:, l * LANES:(l + 1) * LANES]
    for l in range(n_slabs):
        rows = [stage_ref[l, pl.ds(k // SUBLANES + SEG * (k % SUBLANES), SUBLANES, stride=SUBLANES), :]
                for k in range(SEG)]
        o_ref[:, l * LANES:(l + 1) * LANES] += jnp.concatenate(rows, axis=0)


def _out_ffn(x, attn, ssm, g1, pmw, pfw, sc2, sh2, g2, pow_, w_out, w_up, cw, cb, w_down):
    tm = TOKEN_TILE
    row = lambda i: (i, 0)
    fixed = lambda i: (0, 0)
    vec = pl.BlockSpec((1, D_MODEL), fixed)
    resident = lambda a: pl.BlockSpec(a.shape, fixed, pipeline_mode=pl.Buffered(1))
    return pl.pallas_call(
        _ffn_kernel,
        grid=(SEQ // tm,),
        in_specs=[
            pl.BlockSpec((None, tm, D_MODEL), lambda i: (0, i, 0)),
            pl.BlockSpec((tm, ATTN_WIDTH), row),
            pl.BlockSpec((tm, SSM_WIDTH), row),
            vec, vec, vec, vec, vec, vec, vec,
            resident(w_out), resident(w_up), resident(cw), resident(cb), resident(w_down),
        ],
        out_specs=pl.BlockSpec((None, tm, D_MODEL), lambda i: (0, i, 0)),
        out_shape=jax.ShapeDtypeStruct((1, SEQ, D_MODEL), F32),
        scratch_shapes=[
            pltpu.VMEM((tm, D_MODEL), BF16),
            pltpu.VMEM((tm, D_MODEL), F32),
            pltpu.VMEM((D_MODEL // LANES, SUBLANES * SEG_PITCH, LANES), F32),
            pltpu.VMEM(((FFN_CONV - 1) * SUBLANES, 2 * D_FF), F32),
        ],
        compiler_params=pltpu.CompilerParams(
            dimension_semantics=("arbitrary",), vmem_limit_bytes=VMEM_LIMIT_BYTES),
        name="ffn",
    )(x, attn, ssm, g1, pmw, pfw, sc2, sh2, g2, pow_, w_out, w_up, cw, cb, w_down)


def _pad_lanes(a):
    return jnp.pad(a, ((0, 0), (0, LANES - a.shape[-1])))


def kernel(x, c, rel_bias, w_ada, b_ada, pre_mix_w, w_in, attn_sinks, ssm_conv_w, ssm_conv_b,
           ssm_dt_bias, ssm_a_log, ssm_d, ssm_norm_w, w_out, post_mix_w, pre_ffn_w, w_up,
           ffn_conv_w, ffn_conv_b, w_down, post_ffn_w):
    assert x.shape == (1, SEQ, D_MODEL) and w_ada.shape[0] == 1

    mod = _modulation(c.reshape(D_MODEL, 1), w_ada, b_ada)
    shift1, scale1, gate1, shift2, scale2, gate2 = jnp.split(mod, N_MOD, axis=-1)
    bias = _bias_table(rel_bias)

    q, kv, xs, bc, zs, dt, w_out_b, w_up_b, w_down_b = _in_proj(
        x, pre_mix_w, scale1, shift1, w_in, ssm_conv_w[0], ssm_conv_b, _pad_lanes(ssm_dt_bias),
        w_out, w_up, w_down)

    attn = _attention(attn_sinks, q, kv, bias)
    ssm = _ssd(xs, bc, zs, dt, _pad_lanes(ssm_a_log), jnp.repeat(ssm_d, SSM_HEAD_DIM, axis=1),
               ssm_norm_w)

    return _out_ffn(x, attn, ssm, gate1, post_mix_w, pre_ffn_w, scale2, shift2, gate2, post_ffn_w,
                    w_out_b, w_up_b, ffn_conv_w[0], ffn_conv_b, w_down_b)
```
